```python
import jax, jax.numpy as jnp
from jax import lax
import numpy as np

D_MODEL = 1024
BATCH = 32
SEQ = 2048
DEPTH = 1

N_MEM = 256
EPS = 1e-6
MLSTM_HEADS = 4
MLSTM_HEAD_DIM = D_MODEL // 4
MLSTM_WIDTH = MLSTM_HEADS * MLSTM_HEAD_DIM
QKV_BLOCK = 4
CONV_WIDTH = 4
CHUNK = 64
ATTN_HEADS = 8
ATTN_HEAD_DIM = D_MODEL // 16
ATTN_WIDTH = ATTN_HEADS * ATTN_HEAD_DIM
ROPE_DIM = ATTN_HEAD_DIM // 4
ROPE_THETA = 500000.0
DILATED_PATTERNS = ((128, 1), (512, 4), (2048, 16))
BAND_BLOCK = 128
XATTN_HEADS = 4
XATTN_HEAD_DIM = D_MODEL // 8
XATTN_WIDTH = XATTN_HEADS * XATTN_HEAD_DIM

MIX_WIDTH = MLSTM_WIDTH + ATTN_WIDTH + XATTN_WIDTH
IN_SIZES = [MLSTM_WIDTH] * 3 + [ATTN_WIDTH] * 4 + [XATTN_WIDTH] * 2
IN_WIDTH = sum(IN_SIZES)
IN_OFFSETS = [int(o) for o in np.cumsum(IN_SIZES)[:-1]]

kernel_name = "hymba_mlstm_dilated_memory_block"


def rmsnorm(x, g):
    xf = x.astype(jnp.float32)
    y = xf * lax.rsqrt(jnp.mean(xf * xf, axis=-1, keepdims=True) + EPS)
    return (y * g.astype(jnp.float32)).astype(x.dtype)


def head_layernorm(h, g):
    hf = h.astype(jnp.float32)
    mu = jnp.mean(hf, axis=-1, keepdims=True)
    var = jnp.mean(jnp.square(hf - mu), axis=-1, keepdims=True)
    y = (hf - mu) * lax.rsqrt(var + EPS)
    b, s = h.shape[0], h.shape[1]
    return y.reshape(b, s, -1) * g.astype(jnp.float32)


def partial_rope(x):
    s = x.shape[1]
    half = ROPE_DIM // 2
    pos = jnp.arange(s, dtype=jnp.float32)
    inv = ROPE_THETA ** (-jnp.arange(0, ROPE_DIM, 2, dtype=jnp.float32) / ROPE_DIM)
    ang = pos[:, None] * inv[None, :]
    cos = jnp.cos(ang)[None, :, None, :]
    sin = jnp.sin(ang)[None, :, None, :]
    xf = x.astype(jnp.float32)
    x1, x2, xp = xf[..., :half], xf[..., half:ROPE_DIM], xf[..., ROPE_DIM:]
    out = jnp.concatenate([x1 * cos - x2 * sin, x2 * cos + x1 * sin, xp], axis=-1)
    return out.astype(x.dtype)


def causal_depthwise_conv(x, w, bias):
    out = lax.conv_general_dilated(
        x, w[:, None, :].astype(x.dtype), window_strides=(1,),
        padding=[(CONV_WIDTH - 1, 0)],
        dimension_numbers=('NWC', 'WIO', 'NWC'),
        feature_group_count=x.shape[-1])
    return out + bias.astype(x.dtype)


def banded_causal_attention(q, k, v, steps):
    L, hd = q.shape[-2], q.shape[-1]
    lead = q.shape[:-2]
    nb = -(-L // BAND_BLOCK)
    Lp = nb * BAND_BLOCK
    pad = [(0, 0)] * len(lead) + [(0, Lp - L), (0, 0)]
    qb, kb, vb = (jnp.pad(t.astype(jnp.float32), pad).reshape(*lead, nb, BAND_BLOCK, hd)
                  for t in (q, k, v))

    def with_prev(t):
        prev = jnp.pad(t, [(0, 0)] * len(lead) + [(1, 0), (0, 0), (0, 0)])[..., :-1, :, :]
        return jnp.concatenate([prev, t], axis=-2)

    kw, vw = with_prev(kb), with_prev(vb)
    s = jnp.einsum('...qd,...kd->...qk', qb, kw) * (hd ** -0.5)
    qi = jnp.arange(BAND_BLOCK)[:, None]
    ci = jnp.arange(2 * BAND_BLOCK)[None, :]
    dist = BAND_BLOCK + qi - ci
    key_pos = (jnp.arange(nb)[:, None, None] - 1) * BAND_BLOCK + ci[None]
    valid = (dist >= 0)[None] & (dist <= steps)[None] & (key_pos >= 0)
    s = jnp.where(valid, s, -jnp.inf)
    m = jnp.max(s, axis=-1, keepdims=True)
    p = jnp.exp(s - m)
    l = jnp.sum(p, axis=-1)
    o = jnp.einsum('...qk,...kd->...qd', p, vw) / l[..., None]
    lse = m[..., 0] + jnp.log(l)
    o = o.reshape(*lead, Lp, hd)[..., :L, :]
    lse = lse.reshape(*lead, Lp)[..., :L]
    return o, lse


def dilated_attention(q, k, v):
    b, h, s, hd = q.shape
    outs, lses = [], []
    for window, dil in DILATED_PATTERNS:
        def split(t):
            return t.reshape(b, h, s // dil, dil, hd).swapaxes(2, 3)
        o, lse = banded_causal_attention(split(q), split(k), split(v), window // dil)
        outs.append(o.swapaxes(2, 3).reshape(b, h, s, hd))
        lses.append(lse.swapaxes(2, 3).reshape(b, h, s))
    w = jax.nn.softmax(jnp.stack(lses), axis=0)
    return jnp.einsum('pbhs,pbhsd->bhsd', w, jnp.stack(outs))


def mlstm_chunkwise(q, k, v, i_pre, f_pre):
    b, s, h, dh = q.shape
    nc = s // CHUNK

    def vec_chunks(t):
        return t.astype(jnp.float32).reshape(b, nc, CHUNK, h, dh).transpose(1, 0, 3, 2, 4)

    def gate_chunks(t):
        return t.astype(jnp.float32).reshape(b, nc, CHUNK, h).transpose(1, 0, 3, 2)

    qc, kc, vc = vec_chunks(q), vec_chunks(k), vec_chunks(v)
    li = gate_chunks(i_pre)
    lf = jax.nn.log_sigmoid(gate_chunks(f_pre))
    causal = jnp.tril(jnp.ones((CHUNK, CHUNK), dtype=bool))

    def step(carry, xs):
        C, n, m_prev = carry
        qt, kt, vt, li_c, lf_c = xs
        bcum = jnp.cumsum(lf_c, axis=-1)
        dlog = bcum[..., :, None] - bcum[..., None, :] + li_c[..., None, :]
        dlog = jnp.where(causal, dlog, -jnp.inf)
        a = bcum + m_prev[..., None]
        m = jnp.maximum(a, jnp.max(dlog, axis=-1))
        wts = jnp.exp(dlog - m[..., None]) * jnp.einsum('bhtd,bhsd->bhts', qt, kt)
        inter = jnp.exp(a - m)
        num = inter[..., None] * jnp.einsum('bhtd,bhde->bhte', qt, C) + \
            jnp.einsum('bhts,bhse->bhte', wts, vt)
        den = inter * jnp.einsum('bhtd,bhd->bht', qt, n) + jnp.sum(wts, axis=-1)
        h_out = num / jnp.maximum(jnp.abs(den), jnp.exp(-m))[..., None]
        b_last = bcum[..., -1]
        g_log = b_last[..., None] - bcum + li_c
        m_new = jnp.maximum(b_last + m_prev, jnp.max(g_log, axis=-1))
        g = jnp.exp(g_log - m_new[..., None])
        decay = jnp.exp(b_last + m_prev - m_new)
        C = decay[..., None, None] * C + jnp.einsum('bhsd,bhse->bhde', g[..., None] * kt, vt)
        n = decay[..., None] * n + jnp.einsum('bhs,bhsd->bhd', g, kt)
        return (C, n, m_new), h_out

    init = (jnp.zeros((b, h, dh, dh), jnp.float32), jnp.zeros((b, h, dh), jnp.float32),
            jnp.zeros((b, h), jnp.float32))
    _, hs = lax.scan(step, init, (qc, kc, vc, li, lf))
    return hs.transpose(1, 0, 3, 2, 4).reshape(b, s, h, dh)


def block_diag_proj(t, w):
    b, s, width = t.shape
    tb = t.reshape(b, s, width // QKV_BLOCK, QKV_BLOCK)
    return jnp.einsum('bsnc,ncd->bsnd', tb, w.astype(t.dtype)).reshape(b, s, width)


def hybrid_layer(x, mem, g_norm, w_in, conv_w, conv_b, w_q_blk, w_k_blk, w_v_blk,
                 w_gate, b_gate, g_head, skip, g_mem, w_mem_kv, w_out):
    b, s, _ = x.shape
    hn = rmsnorm(x, g_norm)
    proj = hn @ w_in.astype(hn.dtype)
    xm, zm, om, qa, ka, va, za, qx, zx = jnp.split(proj, IN_OFFSETS, axis=-1)

    xc = jax.nn.silu(causal_depthwise_conv(xm, conv_w, conv_b))
    q_m = block_diag_proj(xc, w_q_blk)
    k_m = block_diag_proj(xc, w_k_blk)
    v_m = block_diag_proj(xm, w_v_blk)
    gates = jnp.concatenate([q_m, k_m, v_m], axis=-1) @ w_gate.astype(q_m.dtype) + \
        b_gate.astype(q_m.dtype)
    i_pre, f_pre = gates[..., :MLSTM_HEADS], gates[..., MLSTM_HEADS:]
    mh = lambda t: t.reshape(b, s, MLSTM_HEADS, MLSTM_HEAD_DIM)
    h_m = mlstm_chunkwise(mh(q_m), mh(k_m) * (MLSTM_HEAD_DIM ** -0.5), mh(v_m), i_pre, f_pre)
    h_m = jax.nn.sigmoid(mh(om).astype(jnp.float32)) * h_m
    y_m = (head_layernorm(h_m, g_head) + skip.astype(jnp.float32) * xc.astype(jnp.float32)) * \
        jax.nn.silu(zm.astype(jnp.float32))
    y_m = y_m.astype(x.dtype)

    ah = lambda t: t.reshape(b, s, ATTN_HEADS, ATTN_HEAD_DIM)
    qa_h = partial_rope(ah(qa)).transpose(0, 2, 1, 3)
    ka_h = partial_rope(ah(ka)).transpose(0, 2, 1, 3)
    va_h = ah(va).transpose(0, 2, 1, 3)
    o_a = dilated_attention(qa_h, ka_h, va_h)
    y_a = o_a.transpose(0, 2, 1, 3).reshape(b, s, ATTN_WIDTH).astype(x.dtype) * jax.nn.silu(za)

    mem_n = rmsnorm(mem, g_mem)
    kv = mem_n @ w_mem_kv.astype(mem_n.dtype)
    kx, vx = kv[..., :XATTN_WIDTH], kv[..., XATTN_WIDTH:]
    qx_h = qx.reshape(b, s, XATTN_HEADS, XATTN_HEAD_DIM).astype(jnp.float32)
    kx_h = kx.reshape(b, -1, XATTN_HEADS, XATTN_HEAD_DIM).astype(jnp.float32)
    vx_h = vx.reshape(b, -1, XATTN_HEADS, XATTN_HEAD_DIM).astype(jnp.float32)
    sc = jnp.einsum('bshd,bmhd->bhsm', qx_h, kx_h) * (XATTN_HEAD_DIM ** -0.5)
    p = jax.nn.softmax(sc, axis=-1)
    o_x = jnp.einsum('bhsm,bmhd->bshd', p, vx_h).reshape(b, s, XATTN_WIDTH)
    y_x = o_x.astype(x.dtype) * jax.nn.silu(zx)

    y = jnp.concatenate([y_m, y_a, y_x], axis=-1) @ w_out.astype(x.dtype)
    return x + y


def setup_inputs(seed: int = 0) -> dict:
    key = jax.random.key(seed)
    ks = jax.random.split(key, 20)
    f32 = jnp.float32
    L, D = DEPTH, D_MODEL
    nblk = MLSTM_WIDTH // QKV_BLOCK
    x = jax.random.normal(ks[0], (BATCH, SEQ, D), f32)
    mem = jax.random.normal(ks[1], (BATCH, N_MEM, D), f32)
    g_norm = 1.0 + 0.01 * jax.random.normal(ks[2], (L, D), f32)
    w_in = jax.random.normal(ks[3], (L, D, IN_WIDTH), f32) * D ** -0.5
    conv_w = jax.random.normal(ks[4], (L, CONV_WIDTH, MLSTM_WIDTH), f32) * CONV_WIDTH ** -0.5
    conv_b = 0.01 * jax.random.normal(ks[5], (L, MLSTM_WIDTH), f32)
    w_q_blk = jax.random.normal(ks[6], (L, nblk, QKV_BLOCK, QKV_BLOCK), f32) * QKV_BLOCK ** -0.5
    w_k_blk = jax.random.normal(ks[7], (L, nblk, QKV_BLOCK, QKV_BLOCK), f32) * QKV_BLOCK ** -0.5
    w_v_blk = jax.random.normal(ks[8], (L, nblk, QKV_BLOCK, QKV_BLOCK), f32) * QKV_BLOCK ** -0.5
    w_gate = jax.random.normal(ks[9], (L, 3 * MLSTM_WIDTH, 2 * MLSTM_HEADS), f32) * \
        (3 * MLSTM_WIDTH) ** -0.5
    b_gate = jnp.concatenate([
        0.1 * jax.random.normal(ks[10], (L, MLSTM_HEADS), f32),
        jax.random.uniform(ks[11], (L, MLSTM_HEADS), f32, 3.0, 6.0)], axis=-1)
    g_head = 1.0 + 0.01 * jax.random.normal(ks[12], (L, MLSTM_WIDTH), f32)
    skip = 1.0 + 0.01 * jax.random.normal(ks[13], (L, MLSTM_WIDTH), f32)
    g_mem = 1.0 + 0.01 * jax.random.normal(ks[14], (L, D), f32)
    w_mem_kv = jax.random.normal(ks[15], (L, D, 2 * XATTN_WIDTH), f32) * D ** -0.5
    w_out = jax.random.normal(ks[16], (L, MIX_WIDTH, D), f32) * MIX_WIDTH ** -0.5
    g_final = 1.0 + 0.01 * jax.random.normal(ks[17], (D,), f32)
    return {"x": x, "mem": mem, "g_norm": g_norm, "w_in": w_in, "conv_w": conv_w,
            "conv_b": conv_b, "w_q_blk": w_q_blk, "w_k_blk": w_k_blk, "w_v_blk": w_v_blk,
            "w_gate": w_gate, "b_gate": b_gate, "g_head": g_head, "skip": skip,
            "g_mem": g_mem, "w_mem_kv": w_mem_kv, "w_out": w_out, "g_final": g_final}


def reference(x, mem, g_norm, w_in, conv_w, conv_b, w_q_blk, w_k_blk, w_v_blk,
              w_gate, b_gate, g_head, skip, g_mem, w_mem_kv, w_out, g_final):
    for l in range(DEPTH):
        x = hybrid_layer(x, mem, g_norm[l], w_in[l], conv_w[l], conv_b[l], w_q_blk[l],
                         w_k_blk[l], w_v_blk[l], w_gate[l], b_gate[l], g_head[l], skip[l],
                         g_mem[l], w_mem_kv[l], w_out[l])
    return rmsnorm(x, g_final)
```

```python
import functools

import jax
import jax.numpy as jnp
from jax import lax
from jax.experimental import pallas as pl
from jax.experimental.pallas import tpu as pltpu

F32 = jnp.float32
BF16 = jnp.bfloat16

D_MODEL = 1024
SEQ = 2048
N_MEM = 256
EPS = 1e-6

MLSTM_HEADS = 4
MLSTM_HEAD_DIM = 256
MLSTM_WIDTH = 1024
QKV_BLOCK = 4
CONV_WIDTH = 4

ATTN_HEADS = 8
ATTN_HEAD_DIM = 64
ATTN_WIDTH = 512
ATTN_PAIRS = ATTN_HEADS // 2
ROPE_DIM = 16
ROPE_THETA = 500000.0
BAND = 128
DILATIONS = (1, 4, 16)

XATTN_HEADS = 4
XATTN_HEAD_DIM = 128
XATTN_WIDTH = 512

MIX_WIDTH = MLSTM_WIDTH + ATTN_WIDTH + XATTN_WIDTH
IN_WIDTH = 3 * MLSTM_WIDTH + 4 * ATTN_WIDTH + 2 * XATTN_WIDTH

LANES = 128
VMEM_LIMIT_BYTES = 56 * 1024 * 1024
NEG = -1e30

ROW_TILE = 512
PROJ_COLS = 512
MLSTM_CHUNK = 256


def _sigmoid(x):
    return 1.0 / (1.0 + jnp.exp(-x))


def _silu(x):
    return x * _sigmoid(x)


def _log_sigmoid(x):
    return jnp.minimum(x, 0.0) - jnp.log(1.0 + jnp.exp(-jnp.abs(x)))


def _dot(a, b):
    return jnp.dot(a, b, preferred_element_type=F32)


def _dot_nt(a, b):
    return lax.dot_general(a, b, (((1,), (1,)), ((), ())), preferred_element_type=F32)


def _dot_tn(a, b):
    return lax.dot_general(a, b, (((0,), (0,)), ((), ())), preferred_element_type=F32)


def _inproj_kernel(x_ref, g_ref, w_ref, pm_ref, qa_ref, ka_ref, va_ref, za_ref, px_ref, h_scr):
    x = x_ref[...]
    ms = jnp.mean(x * x, axis=-1, keepdims=True)
    h_scr[...] = (x * lax.rsqrt(ms + EPS) * g_ref[...]).astype(BF16)

    def proj(col):
        return _dot(h_scr[...], w_ref[:, col:col + PROJ_COLS])

    for n in range(3 * MLSTM_WIDTH // PROJ_COLS):
        pm_ref[:, n * PROJ_COLS:(n + 1) * PROJ_COLS] = proj(n * PROJ_COLS).astype(BF16)
    base = 3 * MLSTM_WIDTH
    for kind, ref in enumerate((qa_ref, ka_ref, va_ref, za_ref)):
        r = proj(base + kind * ATTN_WIDTH).astype(BF16)
        for j in range(ATTN_PAIRS):
            ref[j] = r[:, j * LANES:(j + 1) * LANES]
    base = 3 * MLSTM_WIDTH + 4 * ATTN_WIDTH
    for n in range(2 * XATTN_WIDTH // PROJ_COLS):
        px_ref[:, n * PROJ_COLS:(n + 1) * PROJ_COLS] = proj(base + n * PROJ_COLS).astype(BF16)


def _input_projection(x2, g_norm, w_in_b, batch):
    tokens = x2.shape[0]
    tiles_per_seq = SEQ // ROW_TILE
    pair_spec = pl.BlockSpec((None, ATTN_PAIRS, ROW_TILE, LANES),
                             lambda i: (i // tiles_per_seq, 0, i % tiles_per_seq, 0))
    pair_shape = jax.ShapeDtypeStruct((batch, ATTN_PAIRS, SEQ, LANES), BF16)
    return pl.pallas_call(
        _inproj_kernel,
        grid=(tokens // ROW_TILE,),
        in_specs=[
            pl.BlockSpec((ROW_TILE, D_MODEL), lambda i: (i, 0)),
            pl.BlockSpec((1, D_MODEL), lambda i: (0, 0)),
            pl.BlockSpec((D_MODEL, IN_WIDTH), lambda i: (0, 0)),
        ],
        out_specs=[
            pl.BlockSpec((ROW_TILE, 3 * MLSTM_WIDTH), lambda i: (i, 0)),
            pair_spec, pair_spec, pair_spec, pair_spec,
            pl.BlockSpec((ROW_TILE, 2 * XATTN_WIDTH), lambda i: (i, 0)),
        ],
        out_shape=[
            jax.ShapeDtypeStruct((tokens, 3 * MLSTM_WIDTH), BF16),
            pair_shape, pair_shape, pair_shape, pair_shape,
            jax.ShapeDtypeStruct((tokens, 2 * XATTN_WIDTH), BF16),
        ],
        scratch_shapes=[pltpu.VMEM((ROW_TILE, D_MODEL), BF16)],
        compiler_params=pltpu.CompilerParams(
            dimension_semantics=("arbitrary",), vmem_limit_bytes=VMEM_LIMIT_BYTES),
        name="in_projection",
    )(x2, g_norm, w_in_b)


def _mlstm_kernel(xm_ref, zm_ref, om_ref, convw_ref, convb_ref, wq_ref, wk_ref, wv_ref,
                  wg_ref, bg_ref, ghead_ref, skip_ref, out_ref,
                  xbuf, c_scr, n_scr, m_scr):
    L = MLSTM_CHUNK
    dh = MLSTM_HEAD_DIM

    @pl.when(pl.program_id(1) == 0)
    def _():
        xbuf[0:8, :] = jnp.zeros((8, MLSTM_WIDTH), F32)
        c_scr[...] = jnp.zeros_like(c_scr)
        n_scr[...] = jnp.zeros_like(n_scr)
        m_scr[...] = jnp.zeros_like(m_scr)

    xm_b = xm_ref[...]
    xbuf[8:8 + L, :] = xm_b.astype(F32)
    conv = convb_ref[...] + convw_ref[3:4, :] * xbuf[8:8 + L, :]
    conv = conv + convw_ref[2:3, :] * xbuf[7:7 + L, :]
    conv = conv + convw_ref[1:2, :] * xbuf[6:6 + L, :]
    conv = conv + convw_ref[0:1, :] * xbuf[5:5 + L, :]
    xbuf[0:8, :] = xbuf[L:L + 8, :]
    xc = _silu(conv)
    xc_b = xc.astype(BF16)

    q_f, k_f, v_f = [], [], []
    for h in range(MLSTM_HEADS):
        sl = slice(h * dh, (h + 1) * dh)
        q_f.append(_dot(xc_b[:, sl], wq_ref[h]))
        k_f.append(_dot(xc_b[:, sl], wk_ref[h]))
        v_f.append(_dot(xm_b[:, sl], wv_ref[h]))
    q_b = [t.astype(BF16) for t in q_f]
    k_b = [t.astype(BF16) for t in k_f]
    v_b = [t.astype(BF16) for t in v_f]

    qkv_b = jnp.concatenate(q_b + k_b + v_b, axis=1)
    gates = _dot(qkv_b, wg_ref[...]) + bg_ref[...]
    row = lax.broadcasted_iota(jnp.int32, (L, L), 0)
    col = lax.broadcasted_iota(jnp.int32, (L, L), 1)
    causal = row >= col
    tril = causal.astype(F32)
    bcum = jnp.dot(tril, _log_sigmoid(gates), precision=lax.Precision.HIGHEST,
                   preferred_element_type=F32)
    lane = lax.broadcasted_iota(jnp.int32, (L, LANES), 1)
    gcol = jnp.where(lane < MLSTM_HEADS, gates, bcum)
    grow = gcol.T

    k_scale = dh ** -0.5
    for h in range(MLSTM_HEADS):
        sl = slice(h * dh, (h + 1) * dh)
        i_col = gcol[:, h:h + 1]
        b_col = gcol[:, MLSTM_HEADS + h:MLSTM_HEADS + h + 1]
        i_row = grow[h:h + 1, :]
        b_row = grow[MLSTM_HEADS + h:MLSTM_HEADS + h + 1, :]
        m_prev = m_scr[h, 0:1, 0:1]
        c_old = c_scr[h]
        n_old = n_scr[h, 0:1, :]
        ks_f = k_f[h] * k_scale
        ks_b = ks_f.astype(BF16)

        dlog = jnp.where(causal, b_col - b_row + i_row, NEG)
        a = b_col + m_prev
        m = jnp.maximum(a, jnp.max(dlog, axis=1, keepdims=True))
        wts = jnp.exp(dlog - m) * _dot_nt(q_b[h], ks_b)
        inter = jnp.exp(a - m)
        num = inter * _dot(q_b[h], c_old.astype(BF16)) + _dot(wts.astype(BF16), v_b[h])
        den = inter * jnp.sum(q_f[h] * n_old, axis=1, keepdims=True) + \
            jnp.sum(wts, axis=1, keepdims=True)
        h_t = num * (1.0 / jnp.maximum(jnp.abs(den), jnp.exp(-m)))

        b_last = b_col[L - 1:L, :]
        g_log = b_last - b_col + i_col
        m_new = jnp.maximum(b_last + m_prev, jnp.max(g_log, axis=0, keepdims=True))
        gk = jnp.exp(g_log - m_new) * ks_f
        decay = jnp.exp(b_last + m_prev - m_new)
        c_scr[h] = decay * c_old + _dot_tn(gk.astype(BF16), v_b[h])
        n_scr[h, 0:1, :] = decay * n_old + jnp.sum(gk, axis=0, keepdims=True)
        m_scr[h] = jnp.broadcast_to(m_new, (8, LANES))

        hg = _sigmoid(om_ref[:, sl].astype(F32)) * h_t
        mu = jnp.mean(hg, axis=1, keepdims=True)
        cen = hg - mu
        var = jnp.mean(cen * cen, axis=1, keepdims=True)
        ln = cen * lax.rsqrt(var + EPS) * ghead_ref[:, sl]
        y = (ln + skip_ref[:, sl] * xc[:, sl]) * _silu(zm_ref[:, sl].astype(F32))
        out_ref[:, sl] = y.astype(BF16)


def _mlstm(pm, conv_w, conv_b, wq_d, wk_d, wv_d, wg_pad, bg_pad, g_head, skip, batch):
    L = MLSTM_CHUNK
    nc = SEQ // L
    tokens = pm.shape[0]
    col_spec = lambda c: pl.BlockSpec((L, MLSTM_WIDTH), lambda b, i, c=c: (b * nc + i, c))
    full = lambda shape: pl.BlockSpec(shape, lambda b, i: (0,) * len(shape))
    dh = MLSTM_HEAD_DIM
    return pl.pallas_call(
        _mlstm_kernel,
        grid=(batch, nc),
        in_specs=[
            col_spec(0), col_spec(1), col_spec(2),
            full((CONV_WIDTH, MLSTM_WIDTH)), full((1, MLSTM_WIDTH)),
            full((MLSTM_HEADS, dh, dh)), full((MLSTM_HEADS, dh, dh)), full((MLSTM_HEADS, dh, dh)),
            full((3 * MLSTM_WIDTH, LANES)), full((1, LANES)),
            full((1, MLSTM_WIDTH)), full((1, MLSTM_WIDTH)),
        ],
        out_specs=pl.BlockSpec((L, MLSTM_WIDTH), lambda b, i: (b * nc + i, 0)),
        out_shape=jax.ShapeDtypeStruct((tokens, MLSTM_WIDTH), BF16),
        scratch_shapes=[
            pltpu.VMEM((L + 8, MLSTM_WIDTH), F32),
            pltpu.VMEM((MLSTM_HEADS, dh, dh), F32),
            pltpu.VMEM((MLSTM_HEADS, 8, dh), F32),
            pltpu.VMEM((MLSTM_HEADS, 8, LANES), F32),
        ],
        compiler_params=pltpu.CompilerParams(
            dimension_semantics=("arbitrary", "arbitrary"), vmem_limit_bytes=VMEM_LIMIT_BYTES),
        name="mlstm",
    )(pm, pm, pm, conv_w, conv_b, wq_d, wk_d, wv_d, wg_pad, bg_pad, g_head, skip)


def _band_block(q_blk, kk, vv, has_prev):
    nk = kk.shape[0]
    lane = lax.broadcasted_iota(jnp.int32, (BAND, LANES), 1)
    first = lane < ATTN_HEAD_DIM
    q2 = jnp.concatenate([jnp.where(first, q_blk, 0.0), jnp.where(first, 0.0, q_blk)], axis=0)
    s = _dot_nt(q2.astype(BF16), kk.astype(BF16))
    t = lax.broadcasted_iota(jnp.int32, (2 * BAND, nk), 0) % BAND
    c = lax.broadcasted_iota(jnp.int32, (2 * BAND, nk), 1)
    if has_prev:
        valid = (c >= t) & (c <= t + BAND)
    else:
        valid = c <= t
    s = jnp.where(valid, s, NEG)
    m = jnp.max(s, axis=1, keepdims=True)
    p = jnp.exp(s - m)
    l = jnp.sum(p, axis=1, keepdims=True)
    o2 = _dot(p.astype(BF16), vv.astype(BF16))
    o = jnp.where(first, o2[:BAND], o2[BAND:])
    m_p = jnp.where(first, m[:BAND], m[BAND:])
    l_p = jnp.where(first, l[:BAND], l[BAND:])
    return o, m_p, l_p


def _dilated_kernel(q_ref, k_ref, v_ref, z_ref, ra_ref, rb_ref, rc_ref, out_ref,
                    qs, ks, vs, acc_o, acc_m, acc_l):
    nblk = SEQ // BAND
    q_scale = ATTN_HEAD_DIM ** -0.5

    def rope(x, rows):
        return (x * ra_ref[rows, :] + pltpu.roll(x, LANES - ROPE_DIM // 2, 1) * rb_ref[rows, :]
                + pltpu.roll(x, ROPE_DIM // 2, 1) * rc_ref[rows, :])

    def prep(i, carry):
        rows = pl.ds(pl.multiple_of(i * BAND, BAND), BAND)
        qs[rows, :] = rope(q_ref[rows, :].astype(F32), rows) * q_scale
        ks[rows, :] = rope(k_ref[rows, :].astype(F32), rows)
        vs[rows, :] = v_ref[rows, :].astype(F32)
        return carry

    lax.fori_loop(0, nblk, prep, 0)

    def first_pattern(rows, res):
        o, m_p, l_p = res
        acc_o[rows, :] = o
        acc_m[rows, :] = m_p
        acc_l[rows, :] = l_p

    def merge(rows, res):
        o, m_p, l_p = res
        m_old = acc_m[rows, :]
        m_new = jnp.maximum(m_old, m_p)
        w_old = jnp.exp(m_old - m_new)
        w_new = jnp.exp(m_p - m_new)
        acc_o[rows, :] = acc_o[rows, :] * w_old + o * w_new
        acc_l[rows, :] = acc_l[rows, :] * w_old + l_p * w_new
        acc_m[rows, :] = m_new

    rows0 = pl.ds(0, BAND)
    first_pattern(rows0, _band_block(qs[rows0, :], ks[rows0, :], vs[rows0, :], False))

    def d1_body(i, carry):
        rows = pl.ds(pl.multiple_of(i * BAND, BAND), BAND)
        win = pl.ds(pl.multiple_of((i - 1) * BAND, BAND), 2 * BAND)
        first_pattern(rows, _band_block(qs[rows, :], ks[win, :], vs[win, :], True))
        return carry

    lax.fori_loop(1, nblk, d1_body, 0)

    for d in DILATIONS[1:]:
        blocks_per_class = SEQ // (d * BAND)

        def head_body(r, carry, d=d):
            rows = pl.ds(r, BAND, stride=d)
            merge(rows, _band_block(qs[rows, :], ks[rows, :], vs[rows, :], False))
            return carry

        lax.fori_loop(0, d, head_body, 0)

        if blocks_per_class > 1:
            def tail_body(idx, carry, d=d, bpc=blocks_per_class):
                r = idx % d
                i = 1 + idx // d
                rows = pl.ds(r + i * (d * BAND), BAND, stride=d)
                win = pl.ds(r + (i - 1) * (d * BAND), 2 * BAND, stride=d)
                merge(rows, _band_block(qs[rows, :], ks[win, :], vs[win, :], True))
                return carry

            lax.fori_loop(0, d * (blocks_per_class - 1), tail_body, 0)

    def finish(i, carry):
        rows = pl.ds(pl.multiple_of(i * BAND, BAND), BAND)
        o = acc_o[rows, :] * (1.0 / acc_l[rows, :])
        out_ref[rows, :] = (o * _silu(z_ref[rows, :].astype(F32))).astype(BF16)
        return carry

    lax.fori_loop(0, nblk, finish, 0)


def _dilated_attention(qa, ka, va, za, rope_a, rope_b, rope_c, batch):
    slab = pl.BlockSpec((None, None, SEQ, LANES), lambda b, j: (b, j, 0, 0))
    table = pl.BlockSpec((SEQ, LANES), lambda b, j: (0, 0))
    return pl.pallas_call(
        _dilated_kernel,
        grid=(batch, ATTN_PAIRS),
        in_specs=[slab, slab, slab, slab, table, table, table],
        out_specs=slab,
        out_shape=jax.ShapeDtypeStruct((batch, ATTN_PAIRS, SEQ, LANES), BF16),
        scratch_shapes=[pltpu.VMEM((SEQ, LANES), F32) for _ in range(6)],
        compiler_params=pltpu.CompilerParams(
            dimension_semantics=("arbitrary", "arbitrary"), vmem_limit_bytes=VMEM_LIMIT_BYTES),
        name="dilated_attention",
    )(qa, ka, va, za, rope_a, rope_b, rope_c)


def _outproj_kernel(x_ref, ym_ref, ya_ref, px_ref, mem_ref, gmem_ref, wkv_ref, wout_ref,
                    gfin_ref, out_ref, kx_scr, vx_scr):
    @pl.when(pl.program_id(1) == 0)
    def _():
        mem = mem_ref[...]
        ms = jnp.mean(mem * mem, axis=-1, keepdims=True)
        mem_n = (mem * lax.rsqrt(ms + EPS) * gmem_ref[...]).astype(BF16)
        kv = _dot(mem_n, wkv_ref[...])
        kx_scr[...] = kv[:, :XATTN_WIDTH].astype(BF16)
        vx_scr[...] = kv[:, XATTN_WIDTH:].astype(BF16)

    scale = XATTN_HEAD_DIM ** -0.5
    heads = []
    for h in range(XATTN_HEADS):
        sl = slice(h * XATTN_HEAD_DIM, (h + 1) * XATTN_HEAD_DIM)
        s = _dot_nt(px_ref[:, sl], kx_scr[:, sl]) * scale
        m = jnp.max(s, axis=1, keepdims=True)
        p = jnp.exp(s - m)
        l = jnp.sum(p, axis=1, keepdims=True)
        heads.append(_dot(p.astype(BF16), vx_scr[:, sl]) * (1.0 / l))
    o_x = jnp.concatenate(heads, axis=1)
    y_x = (o_x * _silu(px_ref[:, XATTN_WIDTH:].astype(F32))).astype(BF16)
    y_a = jnp.concatenate([ya_ref[j] for j in range(ATTN_PAIRS)], axis=1)

    y = _dot(ym_ref[...], wout_ref[0:MLSTM_WIDTH, :])
    y = y + _dot(y_a, wout_ref[MLSTM_WIDTH:MLSTM_WIDTH + ATTN_WIDTH, :])
    y = y + _dot(y_x, wout_ref[MLSTM_WIDTH + ATTN_WIDTH:, :])
    r = x_ref[...] + y
    ms = jnp.mean(r * r, axis=-1, keepdims=True)
    out_ref[...] = r * lax.rsqrt(ms + EPS) * gfin_ref[...]


def _output_projection(x2, y_m, y_a, px, mem, g_mem, wkv_b, wout_b, g_final, batch):
    tokens = x2.shape[0]
    tps = SEQ // ROW_TILE
    rows = lambda width: pl.BlockSpec((ROW_TILE, width), lambda b, i: (b * tps + i, 0))
    full = lambda shape: pl.BlockSpec(shape, lambda b, i: (0,) * len(shape))
    return pl.pallas_call(
        _outproj_kernel,
        grid=(batch, tps),
        in_specs=[
            rows(D_MODEL), rows(MLSTM_WIDTH),
            pl.BlockSpec((None, ATTN_PAIRS, ROW_TILE, LANES), lambda b, i: (b, 0, i, 0)),
            rows(2 * XATTN_WIDTH),
            pl.BlockSpec((None, N_MEM, D_MODEL), lambda b, i: (b, 0, 0)),
            full((1, D_MODEL)), full((D_MODEL, 2 * XATTN_WIDTH)), full((MIX_WIDTH, D_MODEL)),
            full((1, D_MODEL)),
        ],
        out_specs=rows(D_MODEL),
        out_shape=jax.ShapeDtypeStruct((tokens, D_MODEL), F32),
        scratch_shapes=[pltpu.VMEM((N_MEM, XATTN_WIDTH), BF16), pltpu.VMEM((N_MEM, XATTN_WIDTH), BF16)],
        compiler_params=pltpu.CompilerParams(
            dimension_semantics=("arbitrary", "arbitrary"), vmem_limit_bytes=VMEM_LIMIT_BYTES),
        name="out_projection",
    )(x2, y_m, y_a, px, mem, g_mem, wkv_b, wout_b, g_final)


def _block_diag_dense(w_blk):
    per_head = MLSTM_HEAD_DIM // QKV_BLOCK
    w = w_blk.reshape(MLSTM_HEADS, per_head, QKV_BLOCK, QKV_BLOCK)
    eye = jnp.eye(per_head, dtype=w.dtype)
    dense = jnp.einsum('hncd,nm->hncmd', w, eye)
    return dense.reshape(MLSTM_HEADS, MLSTM_HEAD_DIM, MLSTM_HEAD_DIM).astype(BF16)


def _rope_tables():
    half = ROPE_DIM // 2
    pos = jnp.arange(SEQ, dtype=F32)
    inv = ROPE_THETA ** (-jnp.arange(0, ROPE_DIM, 2, dtype=F32) / ROPE_DIM)
    ang = pos[:, None] * inv[None, :]
    cos, sin = jnp.cos(ang), jnp.sin(ang)
    ones = jnp.ones((SEQ, ATTN_HEAD_DIM - ROPE_DIM), F32)
    zeros = jnp.zeros((SEQ, ATTN_HEAD_DIM - ROPE_DIM), F32)
    zero_h = jnp.zeros((SEQ, half), F32)
    a = jnp.concatenate([cos, cos, ones], axis=1)
    b = jnp.concatenate([-sin, zero_h, zeros], axis=1)
    c = jnp.concatenate([zero_h, sin, zeros], axis=1)
    tile = lambda t: jnp.concatenate([t, t], axis=1)
    return tile(a), tile(b), tile(c)


def kernel(x, mem, g_norm, w_in, conv_w, conv_b, w_q_blk, w_k_blk, w_v_blk, w_gate, b_gate,
           g_head, skip, g_mem, w_mem_kv, w_out, g_final):
    batch = x.shape[0]
    assert x.shape[1:] == (SEQ, D_MODEL) and g_norm.shape[0] == 1
    layer = 0
    x2 = x.reshape(batch * SEQ, D_MODEL)
    pm, qa, ka, va, za, px = _input_projection(
        x2, g_norm[layer][None, :], w_in[layer].astype(BF16), batch)

    wg_pad = jnp.pad(w_gate[layer], ((0, 0), (0, LANES - 2 * MLSTM_HEADS))).astype(BF16)
    bg_pad = jnp.pad(b_gate[layer], (0, LANES - 2 * MLSTM_HEADS))[None, :]
    y_m = _mlstm(pm, conv_w[layer], conv_b[layer][None, :],
                 _block_diag_dense(w_q_blk[layer]), _block_diag_dense(w_k_blk[layer]),
                 _block_diag_dense(w_v_blk[layer]), wg_pad, bg_pad,
                 g_head[layer][None, :], skip[layer][None, :], batch)

    rope_a, rope_b, rope_c = _rope_tables()
    y_a = _dilated_attention(qa, ka, va, za, rope_a, rope_b, rope_c, batch)

    out = _output_projection(x2, y_m, y_a, px, mem, g_mem[layer][None, :],
                             w_mem_kv[layer].astype(BF16), w_out[layer].astype(BF16),
                             g_final[None, :], batch)
    return out.reshape(batch, SEQ, D_MODEL)
```

```python
import functools

import jax
import jax.numpy as jnp
from jax import lax
from jax.experimental import pallas as pl
from jax.experimental.pallas import tpu as pltpu

F32 = jnp.float32
BF16 = jnp.bfloat16

D_MODEL = 1024
SEQ = 2048
N_MEM = 256
EPS = 1e-6

MLSTM_HEADS = 4
MLSTM_HEAD_DIM = 256
MLSTM_WIDTH = 1024
QKV_BLOCK = 4
CONV_WIDTH = 4

ATTN_HEADS = 8
ATTN_HEAD_DIM = 64
ATTN_WIDTH = 512
ATTN_PAIRS = ATTN_HEADS // 2
ROPE_DIM = 16
ROPE_THETA = 500000.0
BAND = 128
DILATIONS = (1, 4, 16)

XATTN_HEADS = 4
XATTN_HEAD_DIM = 128
XATTN_WIDTH = 512

MIX_WIDTH = MLSTM_WIDTH + ATTN_WIDTH + XATTN_WIDTH
IN_WIDTH = 3 * MLSTM_WIDTH + 4 * ATTN_WIDTH + 2 * XATTN_WIDTH

LANES = 128
VMEM_LIMIT_BYTES = 56 * 1024 * 1024
NEG = -1e30

ROW_TILE = 512
PROJ_COLS = 512
MLSTM_CHUNK = 256


def _sigmoid(x):
    return 1.0 / (1.0 + jnp.exp(-x))


def _silu(x):
    return x * _sigmoid(x)


def _log_sigmoid(x):
    return jnp.minimum(x, 0.0) - jnp.log(1.0 + jnp.exp(-jnp.abs(x)))


def _dot(a, b):
    return jnp.dot(a, b, preferred_element_type=F32)


def _dot_nt(a, b):
    return lax.dot_general(a, b, (((1,), (1,)), ((), ())), preferred_element_type=F32)


def _dot_tn(a, b):
    return lax.dot_general(a, b, (((0,), (0,)), ((), ())), preferred_element_type=F32)


def _inproj_kernel(x_ref, g_ref, w_ref, pm_ref, qa_ref, ka_ref, va_ref, za_ref, px_ref, h_scr):
    x = x_ref[...]
    ms = jnp.mean(x * x, axis=-1, keepdims=True)
    h_scr[...] = (x * lax.rsqrt(ms + EPS) * g_ref[...]).astype(BF16)

    def proj(col):
        return _dot(h_scr[...], w_ref[:, col:col + PROJ_COLS])

    for n in range(3 * MLSTM_WIDTH // PROJ_COLS):
        pm_ref[:, n * PROJ_COLS:(n + 1) * PROJ_COLS] = proj(n * PROJ_COLS).astype(BF16)
    base = 3 * MLSTM_WIDTH
    for kind, ref in enumerate((qa_ref, ka_ref, va_ref, za_ref)):
        r = proj(base + kind * ATTN_WIDTH).astype(BF16)
        for j in range(ATTN_PAIRS):
            ref[j] = r[:, j * LANES:(j + 1) * LANES]
    base = 3 * MLSTM_WIDTH + 4 * ATTN_WIDTH
    for n in range(2 * XATTN_WIDTH // PROJ_COLS):
        px_ref[:, n * PROJ_COLS:(n + 1) * PROJ_COLS] = proj(base + n * PROJ_COLS).astype(BF16)


def _input_projection(x2, g_norm, w_in_b, batch):
    tokens = x2.shape[0]
    tiles_per_seq = SEQ // ROW_TILE
    pair_spec = pl.BlockSpec((None, ATTN_PAIRS, ROW_TILE, LANES),
                             lambda i: (i // tiles_per_seq, 0, i % tiles_per_seq, 0))
    pair_shape = jax.ShapeDtypeStruct((batch, ATTN_PAIRS, SEQ, LANES), BF16)
    return pl.pallas_call(
        _inproj_kernel,
        grid=(tokens // ROW_TILE,),
        in_specs=[
            pl.BlockSpec((ROW_TILE, D_MODEL), lambda i: (i, 0)),
            pl.BlockSpec((1, D_MODEL), lambda i: (0, 0)),
            pl.BlockSpec((D_MODEL, IN_WIDTH), lambda i: (0, 0)),
        ],
        out_specs=[
            pl.BlockSpec((ROW_TILE, 3 * MLSTM_WIDTH), lambda i: (i, 0)),
            pair_spec, pair_spec, pair_spec, pair_spec,
            pl.BlockSpec((ROW_TILE, 2 * XATTN_WIDTH), lambda i: (i, 0)),
        ],
        out_shape=[
            jax.ShapeDtypeStruct((tokens, 3 * MLSTM_WIDTH), BF16),
            pair_shape, pair_shape, pair_shape, pair_shape,
            jax.ShapeDtypeStruct((tokens, 2 * XATTN_WIDTH), BF16),
        ],
        scratch_shapes=[pltpu.VMEM((ROW_TILE, D_MODEL), BF16)],
        compiler_params=pltpu.CompilerParams(
            dimension_semantics=("arbitrary",), vmem_limit_bytes=VMEM_LIMIT_BYTES),
        name="in_projection",
    )(x2, g_norm, w_in_b)


def _mlstm_kernel(xm_ref, zm_ref, om_ref, convw_ref, convb_ref, wq_ref, wk_ref, wv_ref,
                  wg_ref, bg_ref, ghead_ref, skip_ref, out_ref,
                  xbuf, c_scr, n_scr, m_scr):
    L = MLSTM_CHUNK
    dh = MLSTM_HEAD_DIM

    @pl.when(pl.program_id(1) == 0)
    def _():
        xbuf[0:8, :] = jnp.zeros((8, MLSTM_WIDTH), F32)
        c_scr[...] = jnp.zeros_like(c_scr)
        n_scr[...] = jnp.zeros_like(n_scr)
        m_scr[...] = jnp.zeros_like(m_scr)

    xm_b = xm_ref[...]
    xbuf[8:8 + L, :] = xm_b.astype(F32)
    conv = convb_ref[...] + convw_ref[3:4, :] * xbuf[8:8 + L, :]
    conv = conv + convw_ref[2:3, :] * xbuf[7:7 + L, :]
    conv = conv + convw_ref[1:2, :] * xbuf[6:6 + L, :]
    conv = conv + convw_ref[0:1, :] * xbuf[5:5 + L, :]
    xbuf[0:8, :] = xbuf[L:L + 8, :]
    xc = _silu(conv)
    xc_b = xc.astype(BF16)

    q_f, k_f, v_f = [], [], []
    for h in range(MLSTM_HEADS):
        sl = slice(h * dh, (h + 1) * dh)
        q_f.append(_dot(xc_b[:, sl], wq_ref[h]))
        k_f.append(_dot(xc_b[:, sl], wk_ref[h]))
        v_f.append(_dot(xm_b[:, sl], wv_ref[h]))
    q_b = [t.astype(BF16) for t in q_f]
    k_b = [t.astype(BF16) for t in k_f]
    v_b = [t.astype(BF16) for t in v_f]

    qkv_b = jnp.concatenate(q_b + k_b + v_b, axis=1)
    gates = _dot(qkv_b, wg_ref[...]) + bg_ref[...]
    row = lax.broadcasted_iota(jnp.int32, (L, L), 0)
    col = lax.broadcasted_iota(jnp.int32, (L, L), 1)
    causal = row >= col
    tril = causal.astype(F32)
    bcum = jnp.dot(tril, _log_sigmoid(gates), precision=lax.Precision.HIGHEST,
                   preferred_element_type=F32)
    lane = lax.broadcasted_iota(jnp.int32, (L, LANES), 1)
    gcol = jnp.where(lane < MLSTM_HEADS, gates, bcum)
    grow = gcol.T

    k_scale = dh ** -0.5
    for h in range(MLSTM_HEADS):
        sl = slice(h * dh, (h + 1) * dh)
        i_col = gcol[:, h:h + 1]
        b_col = gcol[:, MLSTM_HEADS + h:MLSTM_HEADS + h + 1]
        i_row = grow[h:h + 1, :]
        b_row = grow[MLSTM_HEADS + h:MLSTM_HEADS + h + 1, :]
        m_prev = m_scr[h, 0:1, 0:1]
        c_old = c_scr[h]
        n_old = n_scr[h, 0:1, :]
        ks_f = k_f[h] * k_scale
        ks_b = ks_f.astype(BF16)

        dlog = jnp.where(causal, b_col - b_row + i_row, NEG)
        a = b_col + m_prev
        m = jnp.maximum(a, jnp.max(dlog, axis=1, keepdims=True))
        wts = jnp.exp(dlog - m) * _dot_nt(q_b[h], ks_b)
        inter = jnp.exp(a - m)
        num = inter * _dot(q_b[h], c_old.astype(BF16)) + _dot(wts.astype(BF16), v_b[h])
        den = inter * jnp.sum(q_f[h] * n_old, axis=1, keepdims=True) + \
            jnp.sum(wts, axis=1, keepdims=True)
        h_t = num * (1.0 / jnp.maximum(jnp.abs(den), jnp.exp(-m)))

        b_last = b_col[L - 1:L, :]
        g_log = b_last - b_col + i_col
        m_new = jnp.maximum(b_last + m_prev, jnp.max(g_log, axis=0, keepdims=True))
        gk = jnp.exp(g_log - m_new) * ks_f
        decay = jnp.exp(b_last + m_prev - m_new)
        c_scr[h] = decay * c_old + _dot_tn(gk.astype(BF16), v_b[h])
        n_scr[h, 0:1, :] = decay * n_old + jnp.sum(gk, axis=0, keepdims=True)
        m_scr[h] = jnp.broadcast_to(m_new, (8, LANES))

        hg = _sigmoid(om_ref[:, sl].astype(F32)) * h_t
        mu = jnp.mean(hg, axis=1, keepdims=True)
        cen = hg - mu
        var = jnp.mean(cen * cen, axis=1, keepdims=True)
        ln = cen * lax.rsqrt(var + EPS) * ghead_ref[:, sl]
        y = (ln + skip_ref[:, sl] * xc[:, sl]) * _silu(zm_ref[:, sl].astype(F32))
        out_ref[:, sl] = y.astype(BF16)


def _mlstm(pm, conv_w, conv_b, wq_d, wk_d, wv_d, wg_pad, bg_pad, g_head, skip, batch):
    L = MLSTM_CHUNK
    nc = SEQ // L
    tokens = pm.shape[0]
    col_spec = lambda c: pl.BlockSpec((L, MLSTM_WIDTH), lambda b, i, c=c: (b * nc + i, c))
    full = lambda shape: pl.BlockSpec(shape, lambda b, i: (0,) * len(shape))
    dh = MLSTM_HEAD_DIM
    return pl.pallas_call(
        _mlstm_kernel,
        grid=(batch, nc),
        in_specs=[
            col_spec(0), col_spec(1), col_spec(2),
            full((CONV_WIDTH, MLSTM_WIDTH)), full((1, MLSTM_WIDTH)),
            full((MLSTM_HEADS, dh, dh)), full((MLSTM_HEADS, dh, dh)), full((MLSTM_HEADS, dh, dh)),
            full((3 * MLSTM_WIDTH, LANES)), full((1, LANES)),
            full((1, MLSTM_WIDTH)), full((1, MLSTM_WIDTH)),
        ],
        out_specs=pl.BlockSpec((L, MLSTM_WIDTH), lambda b, i: (b * nc + i, 0)),
        out_shape=jax.ShapeDtypeStruct((tokens, MLSTM_WIDTH), BF16),
        scratch_shapes=[
            pltpu.VMEM((L + 8, MLSTM_WIDTH), F32),
            pltpu.VMEM((MLSTM_HEADS, dh, dh), F32),
            pltpu.VMEM((MLSTM_HEADS, 8, dh), F32),
            pltpu.VMEM((MLSTM_HEADS, 8, LANES), F32),
        ],
        compiler_params=pltpu.CompilerParams(
            dimension_semantics=("arbitrary", "arbitrary"), vmem_limit_bytes=VMEM_LIMIT_BYTES),
        name="mlstm",
    )(pm, pm, pm, conv_w, conv_b, wq_d, wk_d, wv_d, wg_pad, bg_pad, g_head, skip)


def _band_block(q_top, q_bot, kk, vv, bias):
    nk = kk.shape[0]
    q2 = jnp.concatenate([q_top, q_bot], axis=0).astype(BF16)
    s = _dot_nt(q2, kk.astype(BF16)) + jnp.concatenate([bias, bias], axis=0)
    m = jnp.max(s, axis=1, keepdims=True)
    p = jnp.exp(s - m).astype(BF16)
    v_aug = jnp.concatenate([vv.astype(BF16), jnp.ones((nk, LANES), BF16)], axis=1)
    o2 = _dot(p, v_aug)
    first = lax.broadcasted_iota(jnp.int32, (BAND, LANES), 1) < ATTN_HEAD_DIM
    o = jnp.where(first, o2[:BAND, :LANES], o2[BAND:, :LANES])
    l_p = jnp.where(first, o2[:BAND, LANES:], o2[BAND:, LANES:])
    m_p = jnp.where(first, m[:BAND], m[BAND:])
    return o, m_p, l_p


def _dilated_kernel(q_ref, k_ref, v_ref, z_ref, ra_ref, rb_ref, rc_ref, out_ref,
                    qs0, qs1, ks, vs, acc_o, acc_m, acc_l, bias_prev, bias_self):
    nblk = SEQ // BAND
    q_scale = ATTN_HEAD_DIM ** -0.5

    t = lax.broadcasted_iota(jnp.int32, (BAND, 2 * BAND), 0)
    c = lax.broadcasted_iota(jnp.int32, (BAND, 2 * BAND), 1)
    bias_prev[...] = jnp.where((c >= t) & (c <= t + BAND), 0.0, NEG)
    t = lax.broadcasted_iota(jnp.int32, (BAND, BAND), 0)
    c = lax.broadcasted_iota(jnp.int32, (BAND, BAND), 1)
    bias_self[...] = jnp.where(c <= t, 0.0, NEG)

    def rope(x, rows):
        return (x * ra_ref[rows, :] + pltpu.roll(x, LANES - ROPE_DIM // 2, 1) * rb_ref[rows, :]
                + pltpu.roll(x, ROPE_DIM // 2, 1) * rc_ref[rows, :])

    def prep(i, carry):
        rows = pl.ds(pl.multiple_of(i * BAND, BAND), BAND)
        q = rope(q_ref[rows, :].astype(F32), rows) * q_scale
        first = lax.broadcasted_iota(jnp.int32, (BAND, LANES), 1) < ATTN_HEAD_DIM
        qs0[rows, :] = jnp.where(first, q, 0.0)
        qs1[rows, :] = jnp.where(first, 0.0, q)
        ks[rows, :] = rope(k_ref[rows, :].astype(F32), rows)
        vs[rows, :] = v_ref[rows, :].astype(F32)
        return carry

    lax.fori_loop(0, nblk, prep, 0)

    def run_blocks(blocks, merge):
        results = []
        for rows, win in blocks:
            if win is None:
                res = _band_block(qs0[rows, :], qs1[rows, :], ks[rows, :], vs[rows, :], bias_self[...])
            else:
                res = _band_block(qs0[rows, :], qs1[rows, :], ks[win, :], vs[win, :], bias_prev[...])
            results.append(res)
        if merge:
            olds = [(acc_o[rows, :], acc_m[rows, :], acc_l[rows, :]) for rows, _ in blocks]
            merged = []
            for (o, m_p, l_p), (o_old, m_old, l_old) in zip(results, olds):
                m_new = jnp.maximum(m_old, m_p)
                w_old = jnp.exp(m_old - m_new)
                w_new = jnp.exp(m_p - m_new)
                merged.append((o_old * w_old + o * w_new, m_new, l_old * w_old + l_p * w_new))
            results = merged
        for (rows, _), (o, m_p, l_p) in zip(blocks, results):
            acc_o[rows, :] = o
            acc_m[rows, :] = m_p
            acc_l[rows, :] = l_p

    run_blocks([(pl.ds(0, BAND), None)], merge=False)
    d1_unroll = 3

    def d1_body(it, carry):
        blocks = []
        for u in range(d1_unroll):
            start = pl.multiple_of((1 + it * d1_unroll + u) * BAND, BAND)
            blocks.append((pl.ds(start, BAND), pl.ds(start - BAND, 2 * BAND)))
        run_blocks(blocks, merge=False)
        return carry

    lax.fori_loop(0, (nblk - 1) // d1_unroll, d1_body, 0)

    d = DILATIONS[1]
    span = d * BAND
    run_blocks([(pl.ds(r, BAND, stride=d), None) for r in range(d)], merge=True)

    def d4_body(i, carry):
        run_blocks([(pl.ds(r + i * span, BAND, stride=d), pl.ds(r + (i - 1) * span, 2 * BAND, stride=d))
                    for r in range(d)], merge=True)
        return carry

    lax.fori_loop(1, SEQ // span, d4_body, 0)

    d = DILATIONS[2]
    d16_unroll = 4

    def d16_body(it, carry):
        run_blocks([(pl.ds(it * d16_unroll + u, BAND, stride=DILATIONS[2]), None)
                    for u in range(d16_unroll)], merge=True)
        return carry

    lax.fori_loop(0, d // d16_unroll, d16_body, 0)

    def finish(i, carry):
        rows = pl.ds(pl.multiple_of(i * BAND, BAND), BAND)
        o = acc_o[rows, :] * (1.0 / acc_l[rows, :])
        out_ref[rows, :] = (o * _silu(z_ref[rows, :].astype(F32))).astype(BF16)
        return carry

    lax.fori_loop(0, nblk, finish, 0)


def _dilated_attention(qa, ka, va, za, rope_a, rope_b, rope_c, batch):
    slab = pl.BlockSpec((None, None, SEQ, LANES), lambda b, j: (b, j, 0, 0))
    table = pl.BlockSpec((SEQ, LANES), lambda b, j: (0, 0))
    return pl.pallas_call(
        _dilated_kernel,
        grid=(batch, ATTN_PAIRS),
        in_specs=[slab, slab, slab, slab, table, table, table],
        out_specs=slab,
        out_shape=jax.ShapeDtypeStruct((batch, ATTN_PAIRS, SEQ, LANES), BF16),
        scratch_shapes=[pltpu.VMEM((SEQ, LANES), F32) for _ in range(7)] + [
            pltpu.VMEM((BAND, 2 * BAND), F32), pltpu.VMEM((BAND, BAND), F32)],
        compiler_params=pltpu.CompilerParams(
            dimension_semantics=("arbitrary", "arbitrary"), vmem_limit_bytes=VMEM_LIMIT_BYTES),
        name="dilated_attention",
    )(qa, ka, va, za, rope_a, rope_b, rope_c)


def _outproj_kernel(x_ref, ym_ref, ya_ref, px_ref, mem_ref, gmem_ref, wkv_ref, wout_ref,
                    gfin_ref, out_ref, kx_scr, vx_scr):
    @pl.when(pl.program_id(1) == 0)
    def _():
        mem = mem_ref[...]
        ms = jnp.mean(mem * mem, axis=-1, keepdims=True)
        mem_n = (mem * lax.rsqrt(ms + EPS) * gmem_ref[...]).astype(BF16)
        kv = _dot(mem_n, wkv_ref[...])
        kx_scr[...] = kv[:, :XATTN_WIDTH].astype(BF16)
        vx_scr[...] = kv[:, XATTN_WIDTH:].astype(BF16)

    scale = XATTN_HEAD_DIM ** -0.5
    heads = []
    for h in range(XATTN_HEADS):
        sl = slice(h * XATTN_HEAD_DIM, (h + 1) * XATTN_HEAD_DIM)
        s = _dot_nt(px_ref[:, sl], kx_scr[:, sl]) * scale
        m = jnp.max(s, axis=1, keepdims=True)
        p = jnp.exp(s - m)
        l = jnp.sum(p, axis=1, keepdims=True)
        heads.append(_dot(p.astype(BF16), vx_scr[:, sl]) * (1.0 / l))
    o_x = jnp.concatenate(heads, axis=1)
    y_x = (o_x * _silu(px_ref[:, XATTN_WIDTH:].astype(F32))).astype(BF16)
    y_a = jnp.concatenate([ya_ref[j] for j in range(ATTN_PAIRS)], axis=1)

    y = _dot(ym_ref[...], wout_ref[0:MLSTM_WIDTH, :])
    y = y + _dot(y_a, wout_ref[MLSTM_WIDTH:MLSTM_WIDTH + ATTN_WIDTH, :])
    y = y + _dot(y_x, wout_ref[MLSTM_WIDTH + ATTN_WIDTH:, :])
    r = x_ref[...] + y
    ms = jnp.mean(r * r, axis=-1, keepdims=True)
    out_ref[...] = r * lax.rsqrt(ms + EPS) * gfin_ref[...]


def _output_projection(x2, y_m, y_a, px, mem, g_mem, wkv_b, wout_b, g_final, batch):
    tokens = x2.shape[0]
    tps = SEQ // ROW_TILE
    rows = lambda width: pl.BlockSpec((ROW_TILE, width), lambda b, i: (b * tps + i, 0))
    full = lambda shape: pl.BlockSpec(shape, lambda b, i: (0,) * len(shape))
    return pl.pallas_call(
        _outproj_kernel,
        grid=(batch, tps),
        in_specs=[
            rows(D_MODEL), rows(MLSTM_WIDTH),
            pl.BlockSpec((None, ATTN_PAIRS, ROW_TILE, LANES), lambda b, i: (b, 0, i, 0)),
            rows(2 * XATTN_WIDTH),
            pl.BlockSpec((None, N_MEM, D_MODEL), lambda b, i: (b, 0, 0)),
            full((1, D_MODEL)), full((D_MODEL, 2 * XATTN_WIDTH)), full((MIX_WIDTH, D_MODEL)),
            full((1, D_MODEL)),
        ],
        out_specs=rows(D_MODEL),
        out_shape=jax.ShapeDtypeStruct((tokens, D_MODEL), F32),
        scratch_shapes=[pltpu.VMEM((N_MEM, XATTN_WIDTH), BF16), pltpu.VMEM((N_MEM, XATTN_WIDTH), BF16)],
        compiler_params=pltpu.CompilerParams(
            dimension_semantics=("arbitrary", "arbitrary"), vmem_limit_bytes=VMEM_LIMIT_BYTES),
        name="out_projection",
    )(x2, y_m, y_a, px, mem, g_mem, wkv_b, wout_b, g_final)


def _block_diag_dense(w_blk):
    per_head = MLSTM_HEAD_DIM // QKV_BLOCK
    w = w_blk.reshape(MLSTM_HEADS, per_head, QKV_BLOCK, QKV_BLOCK)
    eye = jnp.eye(per_head, dtype=w.dtype)
    dense = jnp.einsum('hncd,nm->hncmd', w, eye)
    return dense.reshape(MLSTM_HEADS, MLSTM_HEAD_DIM, MLSTM_HEAD_DIM).astype(BF16)


def _rope_tables():
    half = ROPE_DIM // 2
    pos = jnp.arange(SEQ, dtype=F32)
    inv = ROPE_THETA ** (-jnp.arange(0, ROPE_DIM, 2, dtype=F32) / ROPE_DIM)
    ang = pos[:, None] * inv[None, :]
    cos, sin = jnp.cos(ang), jnp.sin(ang)
    ones = jnp.ones((SEQ, ATTN_HEAD_DIM - ROPE_DIM), F32)
    zeros = jnp.zeros((SEQ, ATTN_HEAD_DIM - ROPE_DIM), F32)
    zero_h = jnp.zeros((SEQ, half), F32)
    a = jnp.concatenate([cos, cos, ones], axis=1)
    b = jnp.concatenate([-sin, zero_h, zeros], axis=1)
    c = jnp.concatenate([zero_h, sin, zeros], axis=1)
    tile = lambda t: jnp.concatenate([t, t], axis=1)
    return tile(a), tile(b), tile(c)


def kernel(x, mem, g_norm, w_in, conv_w, conv_b, w_q_blk, w_k_blk, w_v_blk, w_gate, b_gate,
           g_head, skip, g_mem, w_mem_kv, w_out, g_final):
    batch = x.shape[0]
    assert x.shape[1:] == (SEQ, D_MODEL) and g_norm.shape[0] == 1
    layer = 0
    x2 = x.reshape(batch * SEQ, D_MODEL)
    pm, qa, ka, va, za, px = _input_projection(
        x2, g_norm[layer][None, :], w_in[layer].astype(BF16), batch)

    wg_pad = jnp.pad(w_gate[layer], ((0, 0), (0, LANES - 2 * MLSTM_HEADS))).astype(BF16)
    bg_pad = jnp.pad(b_gate[layer], (0, LANES - 2 * MLSTM_HEADS))[None, :]
    y_m = _mlstm(pm, conv_w[layer], conv_b[layer][None, :],
                 _block_diag_dense(w_q_blk[layer]), _block_diag_dense(w_k_blk[layer]),
                 _block_diag_dense(w_v_blk[layer]), wg_pad, bg_pad,
                 g_head[layer][None, :], skip[layer][None, :], batch)

    rope_a, rope_b, rope_c = _rope_tables()
    y_a = _dilated_attention(qa, ka, va, za, rope_a, rope_b, rope_c, batch)

    out = _output_projection(x2, y_m, y_a, px, mem, g_mem[layer][None, :],
                             w_mem_kv[layer].astype(BF16), w_out[layer].astype(BF16),
                             g_final[None, :], batch)
    return out.reshape(batch, SEQ, D_MODEL)
```

```python
import jax
import jax.numpy as jnp
from jax import lax
from jax.experimental import pallas as pl
from jax.experimental.pallas import tpu as pltpu

F32 = jnp.float32
BF16 = jnp.bfloat16

D_MODEL = 1024
SEQ = 2048
N_MEM = 256
EPS = 1e-6

MLSTM_HEADS = 4
MLSTM_HEAD_DIM = 256
MLSTM_WIDTH = 1024
QKV_BLOCK = 4
CONV_WIDTH = 4

ATTN_HEADS = 8
ATTN_HEAD_DIM = 64
ATTN_WIDTH = 512
ATTN_PAIRS = ATTN_HEADS // 2
ROPE_DIM = 16
ROPE_THETA = 500000.0
BAND = 128
DIL_MID = 4
DIL_MAX = 16

XATTN_HEADS = 4
XATTN_HEAD_DIM = 128
XATTN_WIDTH = 512

MIX_WIDTH = MLSTM_WIDTH + ATTN_WIDTH + XATTN_WIDTH
IN_WIDTH = 3 * MLSTM_WIDTH + 4 * ATTN_WIDTH + 2 * XATTN_WIDTH

LANES = 128
VMEM_LIMIT_BYTES = 56 * 1024 * 1024
NEG = -1e30
LOG2E = 1.4426950408889634

ROW_TILE = 512
PROJ_COLS = 512
MLSTM_CHUNK = 256
MLSTM_SEQS = 2


def _sigmoid(x):
    return 1.0 / (1.0 + jnp.exp(-x))


def _silu(x):
    return x * _sigmoid(x)


def _log_sigmoid(x):
    return jnp.minimum(x, 0.0) - jnp.log(1.0 + jnp.exp(-jnp.abs(x)))


def _dot(a, b):
    return jnp.dot(a, b, preferred_element_type=F32)


def _dot_nt(a, b):
    return lax.dot_general(a, b, (((1,), (1,)), ((), ())), preferred_element_type=F32)


def _dot_tn(a, b):
    return lax.dot_general(a, b, (((0,), (0,)), ((), ())), preferred_element_type=F32)


def _inproj_kernel(x_ref, g_ref, w_ref, pm_ref, qa_ref, ka_ref, va_ref, za_ref, px_ref, h_scr):
    x = x_ref[...]
    ms = jnp.mean(x * x, axis=-1, keepdims=True)
    h_scr[...] = (x * lax.rsqrt(ms + EPS) * g_ref[...]).astype(BF16)

    def proj(col):
        return _dot(h_scr[...], w_ref[:, col:col + PROJ_COLS])

    for n in range(3 * MLSTM_WIDTH // PROJ_COLS):
        pm_ref[:, n * PROJ_COLS:(n + 1) * PROJ_COLS] = proj(n * PROJ_COLS).astype(BF16)
    base = 3 * MLSTM_WIDTH
    for kind, ref in enumerate((qa_ref, ka_ref, va_ref, za_ref)):
        r = proj(base + kind * ATTN_WIDTH).astype(BF16)
        for j in range(ATTN_PAIRS):
            ref[j] = r[:, j * LANES:(j + 1) * LANES]
    base = 3 * MLSTM_WIDTH + 4 * ATTN_WIDTH
    for n in range(2 * XATTN_WIDTH // PROJ_COLS):
        px_ref[:, n * PROJ_COLS:(n + 1) * PROJ_COLS] = proj(base + n * PROJ_COLS).astype(BF16)


def _input_projection(x2, g_norm, w_in_b, batch):
    tokens = x2.shape[0]
    tiles_per_seq = SEQ // ROW_TILE
    pair_spec = pl.BlockSpec((None, ATTN_PAIRS, ROW_TILE, LANES),
                             lambda i: (i // tiles_per_seq, 0, i % tiles_per_seq, 0))
    pair_shape = jax.ShapeDtypeStruct((batch, ATTN_PAIRS, SEQ, LANES), BF16)
    return pl.pallas_call(
        _inproj_kernel,
        grid=(tokens // ROW_TILE,),
        in_specs=[
            pl.BlockSpec((ROW_TILE, D_MODEL), lambda i: (i, 0)),
            pl.BlockSpec((1, D_MODEL), lambda i: (0, 0)),
            pl.BlockSpec((D_MODEL, IN_WIDTH), lambda i: (0, 0), pipeline_mode=pl.Buffered(1)),
        ],
        out_specs=[
            pl.BlockSpec((ROW_TILE, 3 * MLSTM_WIDTH), lambda i: (i, 0)),
            pair_spec, pair_spec, pair_spec, pair_spec,
            pl.BlockSpec((ROW_TILE, 2 * XATTN_WIDTH), lambda i: (i, 0)),
        ],
        out_shape=[
            jax.ShapeDtypeStruct((tokens, 3 * MLSTM_WIDTH), BF16),
            pair_shape, pair_shape, pair_shape, pair_shape,
            jax.ShapeDtypeStruct((tokens, 2 * XATTN_WIDTH), BF16),
        ],
        scratch_shapes=[pltpu.VMEM((ROW_TILE, D_MODEL), BF16)],
        compiler_params=pltpu.CompilerParams(
            dimension_semantics=("arbitrary",), vmem_limit_bytes=VMEM_LIMIT_BYTES),
        name="in_projection",
    )(x2, g_norm, w_in_b)


def _mlstm_chunk(xm_ref, zm_ref, om_ref, convw_ref, convb_ref, wq_ref, wk_ref, wv_ref,
                 wg_ref, bg_ref, ghead_ref, skip_ref, out_ref, xbuf, cn_scr, m_scr):
    L = MLSTM_CHUNK
    dh = MLSTM_HEAD_DIM

    xbuf[8:8 + L, :] = xm_ref[...].astype(F32)
    conv = convb_ref[...] + convw_ref[3:4, :] * xbuf[8:8 + L, :]
    conv = conv + convw_ref[2:3, :] * xbuf[7:7 + L, :]
    conv = conv + convw_ref[1:2, :] * xbuf[6:6 + L, :]
    conv = conv + convw_ref[0:1, :] * xbuf[5:5 + L, :]
    xbuf[0:8, :] = xbuf[L:L + 8, :]
    xc = _silu(conv)
    xc_b = xc.astype(BF16)

    k_scale = dh ** -0.5
    q_b, k_f, k_b, v_b = [], [], [], []
    for h in range(MLSTM_HEADS):
        sl = slice(h * dh, (h + 1) * dh)
        q_b.append(_dot(xc_b[:, sl], wq_ref[h]).astype(BF16))
        k = _dot(xc_b[:, sl], wk_ref[h])
        k_f.append(k * k_scale)
        k_b.append(k.astype(BF16))
        v_b.append(_dot(xm_ref[:, sl], wv_ref[h]).astype(BF16))

    qkv_b = jnp.concatenate(q_b + k_b + v_b, axis=1)
    gates = _dot(qkv_b, wg_ref[...]) + bg_ref[...]
    causal = lax.broadcasted_iota(jnp.int32, (L, L), 0) >= lax.broadcasted_iota(jnp.int32, (L, L), 1)
    b2 = jnp.dot(causal.astype(F32), _log_sigmoid(gates) * LOG2E, precision=lax.Precision.HIGHEST,
                 preferred_element_type=F32)
    w2 = gates * LOG2E - pltpu.roll(b2, LANES - MLSTM_HEADS, 1)
    lane = lax.broadcasted_iota(jnp.int32, (L, LANES), 1)
    gcol = jnp.where(lane < MLSTM_HEADS, w2, b2)
    grow = gcol.T
    ones_b = jnp.ones((L, LANES), BF16)

    for h in range(MLSTM_HEADS):
        sl = slice(h * dh, (h + 1) * dh)
        w_col = gcol[:, h:h + 1]
        b_col = gcol[:, MLSTM_HEADS + h:MLSTM_HEADS + h + 1]
        w_row = grow[h:h + 1, :]
        m_prev = m_scr[h, 0:1, 0:1]
        cn_old = cn_scr[h]
        v_aug = jnp.concatenate([v_b[h], ones_b], axis=1)

        mw = jnp.where(causal, w_row, NEG)
        mm = jnp.maximum(m_prev, jnp.max(mw, axis=1, keepdims=True))
        wts = (jnp.exp2(mw - mm) * _dot_nt(q_b[h], k_f[h].astype(BF16))).astype(BF16)
        inter = jnp.broadcast_to(jnp.exp2(m_prev - mm), (L, LANES))
        inter_c = _dot(q_b[h], cn_old.astype(BF16))
        intra = _dot(wts, v_aug)
        den = inter * inter_c[:, dh:] + intra[:, dh:]
        floor = jnp.broadcast_to(jnp.exp2(-(b_col + mm)), (L, LANES))
        r = 1.0 / jnp.maximum(jnp.abs(den), floor)
        h_t = (jnp.concatenate([inter, inter], axis=1) * inter_c[:, :dh] + intra[:, :dh]) * \
            jnp.concatenate([r, r], axis=1)

        mm_last = mm[L - 1:L, :]
        gk = (jnp.exp2(w_col - mm_last) * k_f[h]).astype(BF16)
        decay = jnp.exp2(m_prev - mm_last)
        cn_scr[h] = decay * cn_old + _dot_tn(gk, v_aug)
        m_scr[h] = jnp.broadcast_to(b_col[L - 1:L, :] + mm_last, (8, LANES))

        hg = _sigmoid(om_ref[:, sl].astype(F32)) * h_t
        mu = jnp.mean(hg, axis=1, keepdims=True)
        cen = hg - mu
        var = jnp.mean(cen * cen, axis=1, keepdims=True)
        ln = cen * lax.rsqrt(var + EPS) * ghead_ref[:, sl]
        y = (ln + skip_ref[:, sl] * xc[:, sl]) * _silu(zm_ref[:, sl].astype(F32))
        out_ref[:, sl] = y.astype(BF16)


def _mlstm_kernel(xm_ref, zm_ref, om_ref, convw_ref, convb_ref, wq_ref, wk_ref, wv_ref,
                  wg_ref, bg_ref, ghead_ref, skip_ref, out_ref, xbuf, cn_scr, m_scr):
    @pl.when(pl.program_id(1) == 0)
    def _():
        xbuf[:, 0:8, :] = jnp.zeros((MLSTM_SEQS, 8, MLSTM_WIDTH), F32)
        cn_scr[...] = jnp.zeros_like(cn_scr)
        m_scr[...] = jnp.zeros_like(m_scr)

    for s in range(MLSTM_SEQS):
        _mlstm_chunk(xm_ref.at[s], zm_ref.at[s], om_ref.at[s], convw_ref, convb_ref, wq_ref, wk_ref,
                     wv_ref, wg_ref, bg_ref, ghead_ref, skip_ref, out_ref.at[s],
                     xbuf.at[s], cn_scr.at[s], m_scr.at[s])


def _mlstm(pm3, conv_w, conv_b, wq_d, wk_d, wv_d, wg_pad, bg_pad, g_head, skip):
    L = MLSTM_CHUNK
    batch = pm3.shape[0]
    assert batch % MLSTM_SEQS == 0
    col_spec = lambda c: pl.BlockSpec((MLSTM_SEQS, L, MLSTM_WIDTH), lambda b, i, c=c: (b, i, c))
    full = lambda shape: pl.BlockSpec(shape, lambda b, i: (0,) * len(shape))
    dh = MLSTM_HEAD_DIM
    return pl.pallas_call(
        _mlstm_kernel,
        grid=(batch // MLSTM_SEQS, SEQ // L),
        in_specs=[
            col_spec(0), col_spec(1), col_spec(2),
            full((CONV_WIDTH, MLSTM_WIDTH)), full((1, MLSTM_WIDTH)),
            full((MLSTM_HEADS, dh, dh)), full((MLSTM_HEADS, dh, dh)), full((MLSTM_HEADS, dh, dh)),
            full((3 * MLSTM_WIDTH, LANES)), full((1, LANES)),
            full((1, MLSTM_WIDTH)), full((1, MLSTM_WIDTH)),
        ],
        out_specs=pl.BlockSpec((MLSTM_SEQS, L, MLSTM_WIDTH), lambda b, i: (b, i, 0)),
        out_shape=jax.ShapeDtypeStruct((batch, SEQ, MLSTM_WIDTH), BF16),
        scratch_shapes=[
            pltpu.VMEM((MLSTM_SEQS, L + 8, MLSTM_WIDTH), F32),
            pltpu.VMEM((MLSTM_SEQS, MLSTM_HEADS, dh, dh + LANES), F32),
            pltpu.VMEM((MLSTM_SEQS, MLSTM_HEADS, 8, LANES), F32),
        ],
        compiler_params=pltpu.CompilerParams(
            dimension_semantics=("arbitrary", "arbitrary"), vmem_limit_bytes=VMEM_LIMIT_BYTES),
        name="mlstm",
    )(pm3, pm3, pm3, conv_w, conv_b, wq_d, wk_d, wv_d, wg_pad, bg_pad, g_head, skip)


def _band_block(q_top, q_bot, kk, vv, bias):
    nk = kk.shape[0]
    q2 = jnp.concatenate([q_top, q_bot], axis=0).astype(BF16)
    s = _dot_nt(q2, kk.astype(BF16)) + jnp.concatenate([bias, bias], axis=0)
    m = jnp.max(s, axis=1, keepdims=True)
    p = jnp.exp2(s - m).astype(BF16)
    v_aug = jnp.concatenate([vv.astype(BF16), jnp.ones((nk, LANES), BF16)], axis=1)
    o2 = _dot(p, v_aug)
    first = lax.broadcasted_iota(jnp.int32, (BAND, LANES), 1) < ATTN_HEAD_DIM
    o = jnp.where(first, o2[:BAND, :LANES], o2[BAND:, :LANES])
    l_p = jnp.where(first, o2[:BAND, LANES:], o2[BAND:, LANES:])
    m_p = jnp.where(first, m[:BAND], m[BAND:])
    return o, m_p, l_p


def _merge(a, b):
    (o_a, m_a, l_a), (o_b, m_b, l_b) = a, b
    m_new = jnp.maximum(m_a, m_b)
    w_a = jnp.exp2(m_a - m_new)
    w_b = jnp.exp2(m_b - m_new)
    return o_a * w_a + o_b * w_b, m_new, l_a * w_a + l_b * w_b


def _dilated_kernel(q_ref, k_ref, v_ref, z_ref, ra_ref, rb_ref, rc_ref, out_ref,
                    q0, q1, ks, vs, q40, q41, k4, v4, a1o, a1m, a1l, a4o, a4m, a4l, zbuf,
                    bias_prev, bias_self):
    nblk = SEQ // BAND
    span = DIL_MID * BAND
    q_scale = ATTN_HEAD_DIM ** -0.5 * LOG2E

    t = lax.broadcasted_iota(jnp.int32, (BAND, 2 * BAND), 0)
    c = lax.broadcasted_iota(jnp.int32, (BAND, 2 * BAND), 1)
    bias_prev[...] = jnp.where((c >= t) & (c <= t + BAND), 0.0, NEG)
    t = lax.broadcasted_iota(jnp.int32, (BAND, BAND), 0)
    c = lax.broadcasted_iota(jnp.int32, (BAND, BAND), 1)
    bias_self[...] = jnp.where(c <= t, 0.0, NEG)

    def rope(x, rows):
        return (x * ra_ref[rows, :] + pltpu.roll(x, LANES - ROPE_DIM // 2, 1) * rb_ref[rows, :]
                + pltpu.roll(x, ROPE_DIM // 2, 1) * rc_ref[rows, :])

    def prep(i, carry):
        rows = pl.ds(pl.multiple_of(i * BAND, BAND), BAND)
        q = rope(q_ref[rows, :].astype(F32), rows) * q_scale
        first = lax.broadcasted_iota(jnp.int32, (BAND, LANES), 1) < ATTN_HEAD_DIM
        q0[rows, :] = jnp.where(first, q, 0.0)
        q1[rows, :] = jnp.where(first, 0.0, q)
        ks[rows, :] = rope(k_ref[rows, :].astype(F32), rows)
        vs[rows, :] = v_ref[rows, :].astype(F32)
        zbuf[rows, :] = _silu(z_ref[rows, :].astype(F32))
        return carry

    lax.fori_loop(0, nblk, prep, 0)

    def deinterleave(idx, carry):
        src = pl.ds(idx // DIL_MID + (idx % DIL_MID) * span, BAND, stride=DIL_MID)
        dst = pl.ds(pl.multiple_of(idx * BAND, BAND), BAND)
        q40[dst, :] = q0[src, :]
        q41[dst, :] = q1[src, :]
        k4[dst, :] = ks[src, :]
        v4[dst, :] = vs[src, :]
        return carry

    lax.fori_loop(0, nblk, deinterleave, 0)

    def attend(qa, qb, kbuf, vbuf, rows, win):
        if win is None:
            return _band_block(qa[rows, :], qb[rows, :], kbuf[rows, :], vbuf[rows, :], bias_self[...])
        return _band_block(qa[rows, :], qb[rows, :], kbuf[win, :], vbuf[win, :], bias_prev[...])

    def put(acc, rows, res):
        for ref, val in zip(acc, res):
            ref[rows, :] = val

    acc1 = (a1o, a1m, a1l)
    acc4 = (a4o, a4m, a4l)

    for qa, qb, kbuf, vbuf, acc, per_class in ((q0, q1, ks, vs, acc1, nblk),
                                                (q40, q41, k4, v4, acc4, nblk // DIL_MID)):
        for b in range(nblk):
            rows = pl.ds(b * BAND, BAND)
            win = None if b % per_class == 0 else pl.ds((b - 1) * BAND, 2 * BAND)
            put(acc, rows, attend(qa, qb, kbuf, vbuf, rows, win))

    for cls in range(DIL_MID):
        for r in range(DIL_MID):
            rows = pl.ds(cls * (SEQ // DIL_MID) + r, BAND, stride=DIL_MID)
            res = attend(q40, q41, k4, v4, rows, None)
            put(acc4, rows, _merge(tuple(ref[rows, :] for ref in acc4), res))

    for b in range(nblk):
        rows4 = pl.ds(b * BAND, BAND)
        rows1 = pl.ds(b // DIL_MID + (b % DIL_MID) * span, BAND, stride=DIL_MID)
        o, _, l = _merge(tuple(ref[rows1, :] for ref in acc1), tuple(ref[rows4, :] for ref in acc4))
        zbuf[rows1, :] = o * (1.0 / l) * zbuf[rows1, :]

    def finish(i, carry):
        rows = pl.ds(pl.multiple_of(i * BAND, BAND), BAND)
        out_ref[rows, :] = zbuf[rows, :].astype(BF16)
        return carry

    lax.fori_loop(0, nblk, finish, 0)


def _dilated_attention(qa, ka, va, za, rope_a, rope_b, rope_c, batch):
    slab = pl.BlockSpec((None, None, SEQ, LANES), lambda b, j: (b, j, 0, 0))
    table = pl.BlockSpec((SEQ, LANES), lambda b, j: (0, 0))
    return pl.pallas_call(
        _dilated_kernel,
        grid=(batch, ATTN_PAIRS),
        in_specs=[slab, slab, slab, slab, table, table, table],
        out_specs=slab,
        out_shape=jax.ShapeDtypeStruct((batch, ATTN_PAIRS, SEQ, LANES), BF16),
        scratch_shapes=[pltpu.VMEM((SEQ, LANES), F32) for _ in range(15)] + [
            pltpu.VMEM((BAND, 2 * BAND), F32), pltpu.VMEM((BAND, BAND), F32)],
        compiler_params=pltpu.CompilerParams(
            dimension_semantics=("arbitrary", "arbitrary"), vmem_limit_bytes=VMEM_LIMIT_BYTES),
        name="dilated_attention",
    )(qa, ka, va, za, rope_a, rope_b, rope_c)


def _outproj_kernel(x_ref, ym_ref, ya_ref, px_ref, mem_ref, gmem_ref, wkv_ref, wout_ref,
                    gfin_ref, out_ref, kx_scr, vx_scr):
    @pl.when(pl.program_id(1) == 0)
    def _():
        mem = mem_ref[...]
        ms = jnp.mean(mem * mem, axis=-1, keepdims=True)
        mem_n = (mem * lax.rsqrt(ms + EPS) * gmem_ref[...]).astype(BF16)
        kv = _dot(mem_n, wkv_ref[...])
        kx_scr[...] = kv[:, :XATTN_WIDTH].astype(BF16)
        vx_scr[...] = kv[:, XATTN_WIDTH:].astype(BF16)

    scale = XATTN_HEAD_DIM ** -0.5
    heads = []
    for h in range(XATTN_HEADS):
        sl = slice(h * XATTN_HEAD_DIM, (h + 1) * XATTN_HEAD_DIM)
        s = _dot_nt(px_ref[:, sl], kx_scr[:, sl]) * scale
        m = jnp.max(s, axis=1, keepdims=True)
        p = jnp.exp(s - m)
        l = jnp.sum(p, axis=1, keepdims=True)
        heads.append(_dot(p.astype(BF16), vx_scr[:, sl]) * (1.0 / l))
    o_x = jnp.concatenate(heads, axis=1)
    y_x = (o_x * _silu(px_ref[:, XATTN_WIDTH:].astype(F32))).astype(BF16)
    y_a = jnp.concatenate([ya_ref[j] for j in range(ATTN_PAIRS)], axis=1)

    y = _dot(ym_ref[...], wout_ref[0:MLSTM_WIDTH, :])
    y = y + _dot(y_a, wout_ref[MLSTM_WIDTH:MLSTM_WIDTH + ATTN_WIDTH, :])
    y = y + _dot(y_x, wout_ref[MLSTM_WIDTH + ATTN_WIDTH:, :])
    r = x_ref[...] + y
    ms = jnp.mean(r * r, axis=-1, keepdims=True)
    out_ref[...] = r * lax.rsqrt(ms + EPS) * gfin_ref[...]


def _output_projection(x2, y_m, y_a, px, mem, g_mem, wkv_b, wout_b, g_final, batch):
    tokens = x2.shape[0]
    tps = SEQ // ROW_TILE
    rows = lambda width: pl.BlockSpec((ROW_TILE, width), lambda b, i: (b * tps + i, 0))
    full = lambda shape: pl.BlockSpec(shape, lambda b, i: (0,) * len(shape))
    return pl.pallas_call(
        _outproj_kernel,
        grid=(batch, tps),
        in_specs=[
            rows(D_MODEL), rows(MLSTM_WIDTH),
            pl.BlockSpec((None, ATTN_PAIRS, ROW_TILE, LANES), lambda b, i: (b, 0, i, 0)),
            rows(2 * XATTN_WIDTH),
            pl.BlockSpec((None, N_MEM, D_MODEL), lambda b, i: (b, 0, 0)),
            full((1, D_MODEL)), full((D_MODEL, 2 * XATTN_WIDTH)), full((MIX_WIDTH, D_MODEL)),
            full((1, D_MODEL)),
        ],
        out_specs=rows(D_MODEL),
        out_shape=jax.ShapeDtypeStruct((tokens, D_MODEL), F32),
        scratch_shapes=[pltpu.VMEM((N_MEM, XATTN_WIDTH), BF16), pltpu.VMEM((N_MEM, XATTN_WIDTH), BF16)],
        compiler_params=pltpu.CompilerParams(
            dimension_semantics=("arbitrary", "arbitrary"), vmem_limit_bytes=VMEM_LIMIT_BYTES),
        name="out_projection",
    )(x2, y_m, y_a, px, mem, g_mem, wkv_b, wout_b, g_final)


def _block_diag_dense(w_blk):
    rows = w_blk.reshape(MLSTM_HEADS, MLSTM_HEAD_DIM, QKV_BLOCK)
    tiled = jnp.tile(rows, (1, 1, MLSTM_HEAD_DIM // QKV_BLOCK))
    idx = jnp.arange(MLSTM_HEAD_DIM) // QKV_BLOCK
    same_block = idx[:, None] == idx[None, :]
    return jnp.where(same_block[None], tiled, 0.0).astype(BF16)


def _rope_tables():
    half = ROPE_DIM // 2
    pos = jnp.arange(SEQ, dtype=F32)
    inv = ROPE_THETA ** (-jnp.arange(0, ROPE_DIM, 2, dtype=F32) / ROPE_DIM)
    ang = pos[:, None] * inv[None, :]
    cos, sin = jnp.cos(ang), jnp.sin(ang)
    ones = jnp.ones((SEQ, ATTN_HEAD_DIM - ROPE_DIM), F32)
    zeros = jnp.zeros((SEQ, ATTN_HEAD_DIM - ROPE_DIM), F32)
    zero_h = jnp.zeros((SEQ, half), F32)
    a = jnp.concatenate([cos, cos, ones], axis=1)
    b = jnp.concatenate([-sin, zero_h, zeros], axis=1)
    c = jnp.concatenate([zero_h, sin, zeros], axis=1)
    tile = lambda t: jnp.concatenate([t, t], axis=1)
    return tile(a), tile(b), tile(c)


def kernel(x, mem, g_norm, w_in, conv_w, conv_b, w_q_blk, w_k_blk, w_v_blk, w_gate, b_gate,
           g_head, skip, g_mem, w_mem_kv, w_out, g_final):
    batch = x.shape[0]
    assert x.shape[1:] == (SEQ, D_MODEL) and g_norm.shape[0] == 1
    layer = 0
    x2 = x.reshape(batch * SEQ, D_MODEL)
    pm, qa, ka, va, za, px = _input_projection(
        x2, g_norm[layer][None, :], w_in[layer].astype(BF16), batch)

    wg_pad = jnp.pad(w_gate[layer], ((0, 0), (0, LANES - 2 * MLSTM_HEADS))).astype(BF16)
    bg_pad = jnp.pad(b_gate[layer], (0, LANES - 2 * MLSTM_HEADS))[None, :]
    y_m = _mlstm(pm.reshape(batch, SEQ, 3 * MLSTM_WIDTH), conv_w[layer], conv_b[layer][None, :],
                 _block_diag_dense(w_q_blk[layer]), _block_diag_dense(w_k_blk[layer]),
                 _block_diag_dense(w_v_blk[layer]), wg_pad, bg_pad,
                 g_head[layer][None, :], skip[layer][None, :])

    rope_a, rope_b, rope_c = _rope_tables()
    y_a = _dilated_attention(qa, ka, va, za, rope_a, rope_b, rope_c, batch)

    out = _output_projection(x2, y_m.reshape(batch * SEQ, MLSTM_WIDTH), y_a, px, mem,
                             g_mem[layer][None, :], w_mem_kv[layer].astype(BF16),
                             w_out[layer].astype(BF16), g_final[None, :], batch)
    return out.reshape(batch, SEQ, D_MODEL)
```

```python
import jax
import jax.numpy as jnp
from jax import lax
from jax.experimental import pallas as pl
from jax.experimental.pallas import tpu as pltpu

F32 = jnp.float32
BF16 = jnp.bfloat16

D_MODEL = 1024
SEQ = 2048
N_MEM = 256
EPS = 1e-6

MLSTM_HEADS = 4
MLSTM_HEAD_DIM = 256
MLSTM_WIDTH = 1024
QKV_BLOCK = 4
CONV_WIDTH = 4

ATTN_HEADS = 8
ATTN_HEAD_DIM = 64
ATTN_WIDTH = 512
ATTN_PAIRS = ATTN_HEADS // 2
ROPE_DIM = 16
ROPE_THETA = 500000.0
BAND = 128
DIL_MID = 4
DIL_MAX = 16

XATTN_HEADS = 4
XATTN_HEAD_DIM = 128
XATTN_WIDTH = 512

MIX_WIDTH = MLSTM_WIDTH + ATTN_WIDTH + XATTN_WIDTH
IN_WIDTH = 3 * MLSTM_WIDTH + 4 * ATTN_WIDTH + 2 * XATTN_WIDTH

LANES = 128
VMEM_LIMIT_BYTES = 56 * 1024 * 1024
NEG = -1e30
LOG2E = 1.4426950408889634

ROW_TILE = 512
PROJ_COLS = 512
MLSTM_CHUNK = 256


def _sigmoid(x):
    return 1.0 / (1.0 + jnp.exp2(x * (-LOG2E)))


def _silu(x):
    return x * _sigmoid(x)


def _log_sigmoid(x):
    return jnp.minimum(x, 0.0) - jnp.log(1.0 + jnp.exp(-jnp.abs(x)))


def _dot(a, b):
    return jnp.dot(a, b, preferred_element_type=F32)


def _dot_nt(a, b):
    return lax.dot_general(a, b, (((1,), (1,)), ((), ())), preferred_element_type=F32)


def _dot_tn(a, b):
    return lax.dot_general(a, b, (((0,), (0,)), ((), ())), preferred_element_type=F32)


def _inproj_kernel(x_ref, g_ref, w_ref, pm_ref, qa_ref, ka_ref, va_ref, za_ref, px_ref, h_scr):
    x = x_ref[...]
    ms = jnp.mean(x * x, axis=-1, keepdims=True)
    h_scr[...] = (x * lax.rsqrt(ms + EPS) * g_ref[...]).astype(BF16)

    def proj(col):
        return _dot(h_scr[...], w_ref[:, col:col + PROJ_COLS])

    for n in range(3 * MLSTM_WIDTH // PROJ_COLS):
        pm_ref[:, n * PROJ_COLS:(n + 1) * PROJ_COLS] = proj(n * PROJ_COLS).astype(BF16)
    base = 3 * MLSTM_WIDTH
    for kind, ref in enumerate((qa_ref, ka_ref, va_ref, za_ref)):
        r = proj(base + kind * ATTN_WIDTH).astype(BF16)
        for j in range(ATTN_PAIRS):
            ref[j] = r[:, j * LANES:(j + 1) * LANES]
    base = 3 * MLSTM_WIDTH + 4 * ATTN_WIDTH
    for n in range(2 * XATTN_WIDTH // PROJ_COLS):
        px_ref[:, n * PROJ_COLS:(n + 1) * PROJ_COLS] = proj(base + n * PROJ_COLS).astype(BF16)


def _input_projection(x2, g_norm, w_in_b, batch):
    tokens = x2.shape[0]
    tiles_per_seq = SEQ // ROW_TILE
    pair_spec = pl.BlockSpec((None, ATTN_PAIRS, ROW_TILE, LANES),
                             lambda i: (i // tiles_per_seq, 0, i % tiles_per_seq, 0))
    pair_shape = jax.ShapeDtypeStruct((batch, ATTN_PAIRS, SEQ, LANES), BF16)
    return pl.pallas_call(
        _inproj_kernel,
        grid=(tokens // ROW_TILE,),
        in_specs=[
            pl.BlockSpec((ROW_TILE, D_MODEL), lambda i: (i, 0)),
            pl.BlockSpec((1, D_MODEL), lambda i: (0, 0)),
            pl.BlockSpec((D_MODEL, IN_WIDTH), lambda i: (0, 0), pipeline_mode=pl.Buffered(1)),
        ],
        out_specs=[
            pl.BlockSpec((ROW_TILE, 3 * MLSTM_WIDTH), lambda i: (i, 0)),
            pair_spec, pair_spec, pair_spec, pair_spec,
            pl.BlockSpec((ROW_TILE, 2 * XATTN_WIDTH), lambda i: (i, 0)),
        ],
        out_shape=[
            jax.ShapeDtypeStruct((tokens, 3 * MLSTM_WIDTH), BF16),
            pair_shape, pair_shape, pair_shape, pair_shape,
            jax.ShapeDtypeStruct((tokens, 2 * XATTN_WIDTH), BF16),
        ],
        scratch_shapes=[pltpu.VMEM((ROW_TILE, D_MODEL), BF16)],
        compiler_params=pltpu.CompilerParams(
            dimension_semantics=("arbitrary",), vmem_limit_bytes=VMEM_LIMIT_BYTES),
        name="in_projection",
    )(x2, g_norm, w_in_b)


def _mlstm_front(xm_ref, convw_ref, convb_ref, wq_ref, wk_ref, wv_ref, wg_ref, bg_ref, xbuf, stage):
    L = MLSTM_CHUNK
    dh = MLSTM_HEAD_DIM
    xc_ref, q_ref, k_ref, v_ref, gcol_ref, grow_ref = stage

    xbuf[8:8 + L, :] = xm_ref[...].astype(F32)
    conv = convb_ref[...] + convw_ref[3:4, :] * xbuf[8:8 + L, :]
    conv = conv + convw_ref[2:3, :] * xbuf[7:7 + L, :]
    conv = conv + convw_ref[1:2, :] * xbuf[6:6 + L, :]
    conv = conv + convw_ref[0:1, :] * xbuf[5:5 + L, :]
    xbuf[0:8, :] = xbuf[L:L + 8, :]
    xc = _silu(conv)
    xc_ref[...] = xc
    xc_b = xc.astype(BF16)
    yield

    k_scale = dh ** -0.5
    q_b, k_b, v_b = [], [], []
    for h in range(MLSTM_HEADS):
        sl = slice(h * dh, (h + 1) * dh)
        q_b.append(_dot(xc_b[:, sl], wq_ref[h]).astype(BF16))
        k = _dot(xc_b[:, sl], wk_ref[h])
        k_ref[:, sl] = k * k_scale
        k_b.append(k.astype(BF16))
        v_b.append(_dot(xm_ref[:, sl], wv_ref[h]).astype(BF16))
        q_ref[:, sl] = q_b[h]
        v_ref[:, sl] = v_b[h]
        if h % 2 == 1:
            yield

    qkv_b = jnp.concatenate(q_b + k_b + v_b, axis=1)
    gates = _dot(qkv_b, wg_ref[...]) + bg_ref[...]
    causal = lax.broadcasted_iota(jnp.int32, (L, L), 0) >= lax.broadcasted_iota(jnp.int32, (L, L), 1)
    b2 = jnp.dot(causal.astype(F32), _log_sigmoid(gates) * LOG2E, precision=lax.Precision.HIGHEST,
                 preferred_element_type=F32)
    w2 = gates * LOG2E - pltpu.roll(b2, LANES - MLSTM_HEADS, 1)
    lane = lax.broadcasted_iota(jnp.int32, (L, LANES), 1)
    gcol = jnp.where(lane < MLSTM_HEADS, w2, b2)
    gcol_ref[...] = gcol
    grow_ref[...] = gcol.T


def _mlstm_back(stage, zm_ref, om_ref, ghead_ref, skip_ref, y_ref, cn_scr, m_scr):
    L = MLSTM_CHUNK
    dh = MLSTM_HEAD_DIM
    xc_ref, q_ref, k_ref, v_ref, gcol_ref, grow_ref = stage
    causal = lax.broadcasted_iota(jnp.int32, (L, L), 0) >= lax.broadcasted_iota(jnp.int32, (L, L), 1)
    ones_b = jnp.ones((L, LANES), BF16)

    for h in range(MLSTM_HEADS):
        sl = slice(h * dh, (h + 1) * dh)
        w_col = gcol_ref[:, h:h + 1]
        b_col = gcol_ref[:, MLSTM_HEADS + h:MLSTM_HEADS + h + 1]
        w_row = grow_ref[h:h + 1, :]
        m_prev = m_scr[h, 0:1, 0:1]
        cn_old = cn_scr[h]
        q_b = q_ref[:, sl]
        k_f = k_ref[:, sl]
        v_aug = jnp.concatenate([v_ref[:, sl], ones_b], axis=1)

        mw = jnp.where(causal, w_row, NEG)
        mm = jnp.maximum(m_prev, jnp.max(mw, axis=1, keepdims=True))
        wts = (jnp.exp2(mw - mm) * _dot_nt(q_b, k_f.astype(BF16))).astype(BF16)
        inter = jnp.broadcast_to(jnp.exp2(m_prev - mm), (L, LANES))
        inter_c = _dot(q_b, cn_old.astype(BF16))
        intra = _dot(wts, v_aug)
        den = inter * inter_c[:, dh:] + intra[:, dh:]
        floor = jnp.broadcast_to(jnp.exp2(-(b_col + mm)), (L, LANES))
        r = 1.0 / jnp.maximum(jnp.abs(den), floor)
        h_t = (jnp.concatenate([inter, inter], axis=1) * inter_c[:, :dh] + intra[:, :dh]) * \
            jnp.concatenate([r, r], axis=1)

        mm_last = mm[L - 1:L, :]
        gk = (jnp.exp2(w_col - mm_last) * k_f).astype(BF16)
        decay = jnp.exp2(m_prev - mm_last)
        cn_scr[h] = decay * cn_old + _dot_tn(gk, v_aug)
        m_scr[h] = jnp.broadcast_to(b_col[L - 1:L, :] + mm_last, (8, LANES))

        hg = _sigmoid(om_ref[:, sl].astype(F32)) * h_t
        mu = jnp.mean(hg, axis=1, keepdims=True)
        cen = hg - mu
        var = jnp.mean(cen * cen, axis=1, keepdims=True)
        ln = cen * lax.rsqrt(var + EPS) * ghead_ref[:, sl]
        y = (ln + skip_ref[:, sl] * xc_ref[:, sl]) * _silu(zm_ref[:, sl].astype(F32))
        y_ref[:, sl] = y.astype(BF16)
        yield


def _mlstm_kernel(xm_a, xm_b, zm_ref, om_ref, convw_ref, convb_ref, wq_ref, wk_ref, wv_ref,
                  wg_ref, bg_ref, ghead_ref, skip_ref, out_ref, xbuf, cn_scr, m_scr, *stages):
    L = MLSTM_CHUNK
    steps_per_seq = SEQ // (2 * L)
    j = pl.program_id(0)
    stage0, stage1 = stages[:len(stages) // 2], stages[len(stages) // 2:]
    front_args = (convw_ref, convb_ref, wq_ref, wk_ref, wv_ref, wg_ref, bg_ref, xbuf)
    back_args = (ghead_ref, skip_ref)

    @pl.when(j == 0)
    def _():
        xbuf[0:8, :] = jnp.zeros((8, MLSTM_WIDTH), F32)
        for ref in stage0:
            ref[...] = jnp.zeros_like(ref)

    @pl.when((j == 0) | (j % steps_per_seq == 1))
    def _():
        cn_scr[...] = jnp.zeros_like(cn_scr)
        m_scr[...] = jnp.zeros_like(m_scr)

    def interleave(front, back):
        for _ in range(MLSTM_HEADS):
            next(front, None)
            next(back, None)
        for _ in front:
            pass

    first, second = pl.ds(0, L), pl.ds(L, L)
    interleave(_mlstm_front(xm_a, *front_args, stage1),
               _mlstm_back(stage0, zm_ref.at[first], om_ref.at[first], *back_args,
                           out_ref.at[first], cn_scr, m_scr))

    @pl.when(j % steps_per_seq == 0)
    def _():
        xbuf[0:8, :] = jnp.zeros((8, MLSTM_WIDTH), F32)

    interleave(_mlstm_front(xm_b, *front_args, stage0),
               _mlstm_back(stage1, zm_ref.at[second], om_ref.at[second], *back_args,
                           out_ref.at[second], cn_scr, m_scr))


def _mlstm(pm, conv_w, conv_b, wq_d, wk_d, wv_d, wg_pad, bg_pad, g_head, skip):
    L = MLSTM_CHUNK
    tokens = pm.shape[0]
    chunks = tokens // L
    assert SEQ % (2 * L) == 0
    dh = MLSTM_HEAD_DIM
    full = lambda shape: pl.BlockSpec(shape, lambda j: (0,) * len(shape))
    pair = lambda col: pl.BlockSpec((2 * L, MLSTM_WIDTH), lambda j: (jnp.maximum(j - 1, 0), col))
    stage = [pltpu.VMEM((L, MLSTM_WIDTH), F32), pltpu.VMEM((L, MLSTM_WIDTH), BF16),
             pltpu.VMEM((L, MLSTM_WIDTH), F32), pltpu.VMEM((L, MLSTM_WIDTH), BF16),
             pltpu.VMEM((L, LANES), F32), pltpu.VMEM((LANES, L), F32)]
    return pl.pallas_call(
        _mlstm_kernel,
        grid=(chunks // 2 + 1,),
        in_specs=[
            pl.BlockSpec((L, MLSTM_WIDTH), lambda j: (jnp.maximum(2 * j - 1, 0), 0)),
            pl.BlockSpec((L, MLSTM_WIDTH), lambda j: (jnp.minimum(2 * j, chunks - 1), 0)),
            pair(1), pair(2),
            full((CONV_WIDTH, MLSTM_WIDTH)), full((1, MLSTM_WIDTH)),
            full((MLSTM_HEADS, dh, dh)), full((MLSTM_HEADS, dh, dh)), full((MLSTM_HEADS, dh, dh)),
            full((3 * MLSTM_WIDTH, LANES)), full((1, LANES)),
            full((1, MLSTM_WIDTH)), full((1, MLSTM_WIDTH)),
        ],
        out_specs=pair(0),
        out_shape=jax.ShapeDtypeStruct((tokens, MLSTM_WIDTH), BF16),
        scratch_shapes=[
            pltpu.VMEM((L + 8, MLSTM_WIDTH), F32),
            pltpu.VMEM((MLSTM_HEADS, dh, dh + LANES), F32),
            pltpu.VMEM((MLSTM_HEADS, 8, LANES), F32),
        ] + stage + stage,
        compiler_params=pltpu.CompilerParams(
            dimension_semantics=("arbitrary",), vmem_limit_bytes=VMEM_LIMIT_BYTES),
        name="mlstm",
    )(pm, pm, pm, pm, conv_w, conv_b, wq_d, wk_d, wv_d, wg_pad, bg_pad, g_head, skip)


def _band_scores(q_top, q_bot, kk, bias):
    q2 = jnp.concatenate([q_top, q_bot], axis=0).astype(BF16)
    return _dot_nt(q2, kk.astype(BF16)) + jnp.concatenate([bias, bias], axis=0)


def _band_output(s, vv):
    nk = vv.shape[0]
    m = jnp.max(s, axis=1, keepdims=True)
    p = jnp.exp2(s - m).astype(BF16)
    v_aug = jnp.concatenate([vv.astype(BF16), jnp.ones((nk, LANES), BF16)], axis=1)
    o2 = _dot(p, v_aug)
    first = lax.broadcasted_iota(jnp.int32, (BAND, LANES), 1) < ATTN_HEAD_DIM
    o = jnp.where(first, o2[:BAND, :LANES], o2[BAND:, :LANES])
    l_p = jnp.where(first, o2[:BAND, LANES:], o2[BAND:, LANES:])
    m_p = jnp.where(first, m[:BAND], m[BAND:])
    return o, m_p, l_p


def _merge(a, b):
    (o_a, m_a, l_a), (o_b, m_b, l_b) = a, b
    m_new = jnp.maximum(m_a, m_b)
    w_a = jnp.exp2(m_a - m_new)
    w_b = jnp.exp2(m_b - m_new)
    return o_a * w_a + o_b * w_b, m_new, l_a * w_a + l_b * w_b


def _dilated_kernel(q_ref, k_ref, v_ref, z_ref, ra_ref, rb_ref, rc_ref, out_ref,
                    q0, q1, ks, vs, q40, q41, k4, v4, a1o, a1m, a1l, a4o, a4m, a4l, zbuf,
                    bias_prev, bias_self):
    nblk = SEQ // BAND
    span = DIL_MID * BAND
    q_scale = ATTN_HEAD_DIM ** -0.5 * LOG2E

    t = lax.broadcasted_iota(jnp.int32, (BAND, 2 * BAND), 0)
    c = lax.broadcasted_iota(jnp.int32, (BAND, 2 * BAND), 1)
    bias_prev[...] = jnp.where((c >= t) & (c <= t + BAND), 0.0, NEG)
    t = lax.broadcasted_iota(jnp.int32, (BAND, BAND), 0)
    c = lax.broadcasted_iota(jnp.int32, (BAND, BAND), 1)
    bias_self[...] = jnp.where(c <= t, 0.0, NEG)

    def rope(x, rows):
        return (x * ra_ref[rows, :] + pltpu.roll(x, LANES - ROPE_DIM // 2, 1) * rb_ref[rows, :]
                + pltpu.roll(x, ROPE_DIM // 2, 1) * rc_ref[rows, :])

    def prep(i, carry):
        rows = pl.ds(pl.multiple_of(i * BAND, BAND), BAND)
        q = rope(q_ref[rows, :].astype(F32), rows) * q_scale
        first = lax.broadcasted_iota(jnp.int32, (BAND, LANES), 1) < ATTN_HEAD_DIM
        q0[rows, :] = jnp.where(first, q, 0.0)
        q1[rows, :] = jnp.where(first, 0.0, q)
        ks[rows, :] = rope(k_ref[rows, :].astype(F32), rows)
        vs[rows, :] = v_ref[rows, :].astype(F32)
        zbuf[rows, :] = _silu(z_ref[rows, :].astype(F32))
        return carry

    lax.fori_loop(0, nblk, prep, 0, unroll=4)

    def deinterleave(idx, carry):
        src = pl.ds(idx // DIL_MID + (idx % DIL_MID) * span, BAND, stride=DIL_MID)
        dst = pl.ds(pl.multiple_of(idx * BAND, BAND), BAND)
        q40[dst, :] = q0[src, :]
        q41[dst, :] = q1[src, :]
        k4[dst, :] = ks[src, :]
        v4[dst, :] = vs[src, :]
        return carry

    lax.fori_loop(0, nblk, deinterleave, 0, unroll=4)

    def put(acc, rows, res):
        for ref, val in zip(acc, res):
            ref[rows, :] = val

    acc1 = (a1o, a1m, a1l)
    acc4 = (a4o, a4m, a4l)

    tasks = []
    for qa, qb, kbuf, vbuf, acc, per_class in ((q0, q1, ks, vs, acc1, nblk),
                                                (q40, q41, k4, v4, acc4, nblk // DIL_MID)):
        for b in range(nblk):
            rows = pl.ds(b * BAND, BAND)
            win = None if b % per_class == 0 else pl.ds((b - 1) * BAND, 2 * BAND)
            tasks.append((qa, qb, kbuf, vbuf, rows, win, acc, False))
    for cls in range(DIL_MID):
        for r in range(DIL_MID):
            rows = pl.ds(cls * (SEQ // DIL_MID) + r, BAND, stride=DIL_MID)
            tasks.append((q40, q41, k4, v4, rows, None, acc4, True))

    def scores(task):
        qa, qb, kbuf, _, rows, win, _, _ = task
        if win is None:
            return _band_scores(qa[rows, :], qb[rows, :], kbuf[rows, :], bias_self[...])
        return _band_scores(qa[rows, :], qb[rows, :], kbuf[win, :], bias_prev[...])

    def complete(task, s):
        _, _, _, vbuf, rows, win, acc, merge = task
        res = _band_output(s, vbuf[rows if win is None else win, :])
        if merge:
            res = _merge(tuple(ref[rows, :] for ref in acc), res)
        put(acc, rows, res)

    lookahead = 4
    pending = []
    for task in tasks:
        pending.append((task, scores(task)))
        if len(pending) > lookahead:
            complete(*pending.pop(0))
    for item in pending:
        complete(*item)

    for b in range(nblk):
        rows4 = pl.ds(b * BAND, BAND)
        rows1 = pl.ds(b // DIL_MID + (b % DIL_MID) * span, BAND, stride=DIL_MID)
        o, _, l = _merge(tuple(ref[rows1, :] for ref in acc1), tuple(ref[rows4, :] for ref in acc4))
        zbuf[rows1, :] = o * (1.0 / l) * zbuf[rows1, :]

    def finish(i, carry):
        rows = pl.ds(pl.multiple_of(i * BAND, BAND), BAND)
        out_ref[rows, :] = zbuf[rows, :].astype(BF16)
        return carry

    lax.fori_loop(0, nblk, finish, 0, unroll=4)


def _dilated_attention(qa, ka, va, za, rope_a, rope_b, rope_c, batch):
    slab = pl.BlockSpec((None, None, SEQ, LANES), lambda b, j: (b, j, 0, 0))
    table = pl.BlockSpec((SEQ, LANES), lambda b, j: (0, 0))
    return pl.pallas_call(
        _dilated_kernel,
        grid=(batch, ATTN_PAIRS),
        in_specs=[slab, slab, slab, slab, table, table, table],
        out_specs=slab,
        out_shape=jax.ShapeDtypeStruct((batch, ATTN_PAIRS, SEQ, LANES), BF16),
        scratch_shapes=[pltpu.VMEM((SEQ, LANES), F32) for _ in range(15)] + [
            pltpu.VMEM((BAND, 2 * BAND), F32), pltpu.VMEM((BAND, BAND), F32)],
        compiler_params=pltpu.CompilerParams(
            dimension_semantics=("arbitrary", "arbitrary"), vmem_limit_bytes=VMEM_LIMIT_BYTES),
        name="dilated_attention",
    )(qa, ka, va, za, rope_a, rope_b, rope_c)


def _outproj_kernel(x_ref, ym_ref, ya_ref, px_ref, mem_ref, gmem_ref, wkv_ref, wout_ref,
                    gfin_ref, out_ref, kx_scr, vx_scr):
    @pl.when(pl.program_id(1) == 0)
    def _():
        mem = mem_ref[...]
        ms = jnp.mean(mem * mem, axis=-1, keepdims=True)
        mem_n = (mem * lax.rsqrt(ms + EPS) * gmem_ref[...]).astype(BF16)
        kv = _dot(mem_n, wkv_ref[...])
        kx_scr[...] = kv[:, :XATTN_WIDTH].astype(BF16)
        vx_scr[...] = kv[:, XATTN_WIDTH:].astype(BF16)

    scale = XATTN_HEAD_DIM ** -0.5 * LOG2E
    ones_b = jnp.ones((N_MEM, LANES), BF16)
    y_a = jnp.concatenate([ya_ref[j] for j in range(ATTN_PAIRS)], axis=1)
    y = _dot(y_a, wout_ref[MLSTM_WIDTH:MLSTM_WIDTH + ATTN_WIDTH, :])
    heads = []
    for h in range(XATTN_HEADS):
        sl = slice(h * XATTN_HEAD_DIM, (h + 1) * XATTN_HEAD_DIM)
        s = _dot_nt(px_ref[:, sl], kx_scr[:, sl]) * scale
        m = jnp.max(s, axis=1, keepdims=True)
        p = jnp.exp2(s - m).astype(BF16)
        o2 = _dot(p, jnp.concatenate([vx_scr[:, sl], ones_b], axis=1))
        heads.append(o2[:, :LANES] * (1.0 / o2[:, LANES:]))
        rows_m = slice(h * MLSTM_HEAD_DIM, (h + 1) * MLSTM_HEAD_DIM)
        y = y + _dot(ym_ref[:, rows_m], wout_ref[rows_m, :])
    o_x = jnp.concatenate(heads, axis=1)
    y_x = (o_x * _silu(px_ref[:, XATTN_WIDTH:].astype(F32))).astype(BF16)
    y = y + _dot(y_x, wout_ref[MLSTM_WIDTH + ATTN_WIDTH:, :])
    r = x_ref[...] + y
    ms = jnp.mean(r * r, axis=-1, keepdims=True)
    out_ref[...] = r * lax.rsqrt(ms + EPS) * gfin_ref[...]


def _output_projection(x2, y_m, y_a, px, mem, g_mem, wkv_b, wout_b, g_final, batch):
    tokens = x2.shape[0]
    tps = SEQ // ROW_TILE
    rows = lambda width: pl.BlockSpec((ROW_TILE, width), lambda b, i: (b * tps + i, 0))
    full = lambda shape: pl.BlockSpec(shape, lambda b, i: (0,) * len(shape))
    return pl.pallas_call(
        _outproj_kernel,
        grid=(batch, tps),
        in_specs=[
            rows(D_MODEL), rows(MLSTM_WIDTH),
            pl.BlockSpec((None, ATTN_PAIRS, ROW_TILE, LANES), lambda b, i: (b, 0, i, 0)),
            rows(2 * XATTN_WIDTH),
            pl.BlockSpec((None, N_MEM, D_MODEL), lambda b, i: (b, 0, 0)),
            full((1, D_MODEL)), full((D_MODEL, 2 * XATTN_WIDTH)), full((MIX_WIDTH, D_MODEL)),
            full((1, D_MODEL)),
        ],
        out_specs=rows(D_MODEL),
        out_shape=jax.ShapeDtypeStruct((tokens, D_MODEL), F32),
        scratch_shapes=[pltpu.VMEM((N_MEM, XATTN_WIDTH), BF16), pltpu.VMEM((N_MEM, XATTN_WIDTH), BF16)],
        compiler_params=pltpu.CompilerParams(
            dimension_semantics=("arbitrary", "arbitrary"), vmem_limit_bytes=VMEM_LIMIT_BYTES),
        name="out_projection",
    )(x2, y_m, y_a, px, mem, g_mem, wkv_b, wout_b, g_final)


def _block_diag_dense(w_blk):
    rows = w_blk.reshape(MLSTM_HEADS, MLSTM_HEAD_DIM, QKV_BLOCK)
    tiled = jnp.tile(rows, (1, 1, MLSTM_HEAD_DIM // QKV_BLOCK))
    idx = jnp.arange(MLSTM_HEAD_DIM) // QKV_BLOCK
    same_block = idx[:, None] == idx[None, :]
    return jnp.where(same_block[None], tiled, 0.0).astype(BF16)


def _rope_tables():
    half = ROPE_DIM // 2
    pos = jnp.arange(SEQ, dtype=F32)
    inv = ROPE_THETA ** (-jnp.arange(0, ROPE_DIM, 2, dtype=F32) / ROPE_DIM)
    ang = pos[:, None] * inv[None, :]
    cos, sin = jnp.cos(ang), jnp.sin(ang)
    ones = jnp.ones((SEQ, ATTN_HEAD_DIM - ROPE_DIM), F32)
    zeros = jnp.zeros((SEQ, ATTN_HEAD_DIM - ROPE_DIM), F32)
    zero_h = jnp.zeros((SEQ, half), F32)
    a = jnp.concatenate([cos, cos, ones], axis=1)
    b = jnp.concatenate([-sin, zero_h, zeros], axis=1)
    c = jnp.concatenate([zero_h, sin, zeros], axis=1)
    tile = lambda t: jnp.concatenate([t, t], axis=1)
    return tile(a), tile(b), tile(c)


def kernel(x, mem, g_norm, w_in, conv_w, conv_b, w_q_blk, w_k_blk, w_v_blk, w_gate, b_gate,
           g_head, skip, g_mem, w_mem_kv, w_out, g_final):
    batch = x.shape[0]
    assert x.shape[1:] == (SEQ, D_MODEL) and g_norm.shape[0] == 1
    layer = 0
    x2 = x.reshape(batch * SEQ, D_MODEL)
    pm, qa, ka, va, za, px = _input_projection(
        x2, g_norm[layer][None, :], w_in[layer].astype(BF16), batch)

    wg_pad = jnp.pad(w_gate[layer], ((0, 0), (0, LANES - 2 * MLSTM_HEADS))).astype(BF16)
    bg_pad = jnp.pad(b_gate[layer], (0, LANES - 2 * MLSTM_HEADS))[None, :]
    y_m = _mlstm(pm, conv_w[layer], conv_b[layer][None, :],
                 _block_diag_dense(w_q_blk[layer]), _block_diag_dense(w_k_blk[layer]),
                 _block_diag_dense(w_v_blk[layer]), wg_pad, bg_pad,
                 g_head[layer][None, :], skip[layer][None, :])

    rope_a, rope_b, rope_c = _rope_tables()
    y_a = _dilated_attention(qa, ka, va, za, rope_a, rope_b, rope_c, batch)

    out = _output_projection(x2, y_m, y_a, px, mem,
                             g_mem[layer][None, :], w_mem_kv[layer].astype(BF16),
                             w_out[layer].astype(BF16), g_final[None, :], batch)
    return out.reshape(batch, SEQ, D_MODEL)
```

```python
import jax
import jax.numpy as jnp
from jax import lax
from jax.experimental import pallas as pl
from jax.experimental.pallas import tpu as pltpu

F32 = jnp.float32
BF16 = jnp.bfloat16

D_MODEL = 1024
SEQ = 2048
N_MEM = 256
EPS = 1e-6

MLSTM_HEADS = 4
MLSTM_HEAD_DIM = 256
MLSTM_WIDTH = 1024
QKV_BLOCK = 4
CONV_WIDTH = 4

ATTN_HEADS = 8
ATTN_HEAD_DIM = 64
ATTN_WIDTH = 512
ATTN_PAIRS = ATTN_HEADS // 2
ROPE_DIM = 16
ROPE_THETA = 500000.0
BAND = 128
DIL_MID = 4
DIL_MAX = 16

XATTN_HEADS = 4
XATTN_HEAD_DIM = 128
XATTN_WIDTH = 512

MIX_WIDTH = MLSTM_WIDTH + ATTN_WIDTH + XATTN_WIDTH
IN_WIDTH = 3 * MLSTM_WIDTH + 4 * ATTN_WIDTH + 2 * XATTN_WIDTH

LANES = 128
VMEM_LIMIT_BYTES = 56 * 1024 * 1024
NEG = -1e30
LOG2E = 1.4426950408889634

ROW_TILE = 512
PROJ_COLS = 512
MLSTM_CHUNK = 256


def _sigmoid(x):
    return 1.0 / (1.0 + jnp.exp2(x * (-LOG2E)))


def _silu(x):
    return x * _sigmoid(x)


def _log_sigmoid(x):
    return jnp.minimum(x, 0.0) - jnp.log(1.0 + jnp.exp(-jnp.abs(x)))


def _dot(a, b):
    return jnp.dot(a, b, preferred_element_type=F32)


def _dot_nt(a, b):
    return lax.dot_general(a, b, (((1,), (1,)), ((), ())), preferred_element_type=F32)


def _dot_tn(a, b):
    return lax.dot_general(a, b, (((0,), (0,)), ((), ())), preferred_element_type=F32)


def _mlstm_front(xm_ref, convw_ref, convb_ref, wq_ref, wk_ref, wv_ref, wg_ref, bg_ref, xbuf, stage):
    L = MLSTM_CHUNK
    dh = MLSTM_HEAD_DIM
    xc_ref, q_ref, k_ref, v_ref, gcol_ref, grow_ref = stage

    xbuf[8:8 + L, :] = xm_ref[...].astype(F32)
    conv = convb_ref[...] + convw_ref[3:4, :] * xbuf[8:8 + L, :]
    conv = conv + convw_ref[2:3, :] * xbuf[7:7 + L, :]
    conv = conv + convw_ref[1:2, :] * xbuf[6:6 + L, :]
    conv = conv + convw_ref[0:1, :] * xbuf[5:5 + L, :]
    xbuf[0:8, :] = xbuf[L:L + 8, :]
    xc = _silu(conv)
    xc_ref[...] = xc
    xc_b = xc.astype(BF16)
    yield

    k_scale = dh ** -0.5
    q_b, k_b, v_b = [], [], []
    for h in range(MLSTM_HEADS):
        sl = slice(h * dh, (h + 1) * dh)
        q_b.append(_dot(xc_b[:, sl], wq_ref[h]).astype(BF16))
        k = _dot(xc_b[:, sl], wk_ref[h])
        k_ref[:, sl] = k * k_scale
        k_b.append(k.astype(BF16))
        v_b.append(_dot(xm_ref[:, sl], wv_ref[h]).astype(BF16))
        q_ref[:, sl] = q_b[h]
        v_ref[:, sl] = v_b[h]
        yield

    qkv_b = jnp.concatenate(q_b + k_b + v_b, axis=1)
    gates = _dot(qkv_b, wg_ref[...]) + bg_ref[...]
    causal = lax.broadcasted_iota(jnp.int32, (L, L), 0) >= lax.broadcasted_iota(jnp.int32, (L, L), 1)
    b2 = jnp.dot(causal.astype(F32), _log_sigmoid(gates) * LOG2E, precision=lax.Precision.HIGHEST,
                 preferred_element_type=F32)
    w2 = gates * LOG2E - pltpu.roll(b2, LANES - MLSTM_HEADS, 1)
    lane = lax.broadcasted_iota(jnp.int32, (L, LANES), 1)
    gcol = jnp.where(lane < MLSTM_HEADS, w2, b2)
    gcol_ref[...] = gcol
    grow_ref[...] = gcol.T


def _mlstm_back(stage, zm_ref, om_ref, ghead_ref, skip_ref, y_ref, cn_scr, m_scr):
    L = MLSTM_CHUNK
    dh = MLSTM_HEAD_DIM
    xc_ref, q_ref, k_ref, v_ref, gcol_ref, grow_ref = stage
    causal = lax.broadcasted_iota(jnp.int32, (L, L), 0) >= lax.broadcasted_iota(jnp.int32, (L, L), 1)
    ones_b = jnp.ones((L, LANES), BF16)

    for h in range(MLSTM_HEADS):
        sl = slice(h * dh, (h + 1) * dh)
        w_col = gcol_ref[:, h:h + 1]
        b_col = gcol_ref[:, MLSTM_HEADS + h:MLSTM_HEADS + h + 1]
        w_row = grow_ref[h:h + 1, :]
        m_prev = m_scr[h, 0:1, 0:1]
        cn_old = cn_scr[h]
        q_b = q_ref[:, sl]
        k_f = k_ref[:, sl]
        v_aug = jnp.concatenate([v_ref[:, sl], ones_b], axis=1)

        mw = jnp.where(causal, w_row, NEG)
        mm = jnp.maximum(m_prev, jnp.max(mw, axis=1, keepdims=True))
        wts = (jnp.exp2(mw - mm) * _dot_nt(q_b, k_f.astype(BF16))).astype(BF16)
        inter = jnp.broadcast_to(jnp.exp2(m_prev - mm), (L, LANES))
        inter_c = _dot(q_b, cn_old.astype(BF16))
        intra = _dot(wts, v_aug)
        den = inter * inter_c[:, dh:] + intra[:, dh:]
        floor = jnp.broadcast_to(jnp.exp2(-(b_col + mm)), (L, LANES))
        r = 1.0 / jnp.maximum(jnp.abs(den), floor)
        h_t = (jnp.concatenate([inter, inter], axis=1) * inter_c[:, :dh] + intra[:, :dh]) * \
            jnp.concatenate([r, r], axis=1)

        mm_last = mm[L - 1:L, :]
        gk = (jnp.exp2(w_col - mm_last) * k_f).astype(BF16)
        decay = jnp.exp2(m_prev - mm_last)
        cn_scr[h] = decay * cn_old + _dot_tn(gk, v_aug)
        m_scr[h] = jnp.broadcast_to(b_col[L - 1:L, :] + mm_last, (8, LANES))
        yield

        hg = _sigmoid(om_ref[:, sl].astype(F32)) * h_t
        mu = jnp.mean(hg, axis=1, keepdims=True)
        cen = hg - mu
        var = jnp.mean(cen * cen, axis=1, keepdims=True)
        ln = cen * lax.rsqrt(var + EPS) * ghead_ref[:, sl]
        y = (ln + skip_ref[:, sl] * xc_ref[:, sl]) * _silu(zm_ref[:, sl].astype(F32))
        y_ref[:, sl] = y.astype(BF16)
        yield


def _round_robin(*gens):
    done = object()
    live = list(gens)
    while live:
        live = [g for g in live if next(g, done) is not done]


def _inproj_pieces(h_scr, w_ref, sinks):
    for col, store in sinks:
        store(_dot(h_scr[...], w_ref[:, col:col + PROJ_COLS]).astype(BF16))
        yield


def _inproj_mlstm_kernel(x_ref, g_ref, w_ref, convw_ref, convb_ref, wq_ref, wk_ref, wv_ref,
                         wg_ref, bg_ref, ghead_ref, skip_ref,
                         qa_ref, ka_ref, va_ref, za_ref, px_ref, ym_ref,
                         h_scr, pm_cur, xm_prev, zo_prev, xbuf, cn_scr, m_scr, *stages):
    L = MLSTM_CHUNK
    steps_per_seq = SEQ // (2 * L)
    j = pl.program_id(0)
    stage0, stage1 = stages[:len(stages) // 2], stages[len(stages) // 2:]
    front_args = (convw_ref, convb_ref, wq_ref, wk_ref, wv_ref, wg_ref, bg_ref, xbuf)
    back_args = (ghead_ref, skip_ref)

    @pl.when(j == 0)
    def _():
        xbuf[0:8, :] = jnp.zeros((8, MLSTM_WIDTH), F32)
        xm_prev[...] = jnp.zeros_like(xm_prev)
        zo_prev[...] = jnp.zeros_like(zo_prev)
        for ref in stage0:
            ref[...] = jnp.zeros_like(ref)

    @pl.when((j == 0) | (j % steps_per_seq == 1))
    def _():
        cn_scr[...] = jnp.zeros_like(cn_scr)
        m_scr[...] = jnp.zeros_like(m_scr)

    x = x_ref[...]
    ms = jnp.mean(x * x, axis=-1, keepdims=True)
    h_scr[...] = (x * lax.rsqrt(ms + EPS) * g_ref[...]).astype(BF16)

    def pm_sink(col):
        def store(r):
            pm_cur[:, col:col + PROJ_COLS] = r
        return store

    def pair_sink(ref, col):
        def store(r):
            for p in range(PROJ_COLS // LANES):
                ref[col // LANES + p] = r[:, p * LANES:(p + 1) * LANES]
        return store

    def px_sink(col):
        def store(r):
            px_ref[:, col:col + PROJ_COLS] = r
        return store

    base_a = 3 * MLSTM_WIDTH
    base_x = base_a + 4 * ATTN_WIDTH
    sinks_a = [(n * PROJ_COLS, pm_sink(n * PROJ_COLS)) for n in range(3 * MLSTM_WIDTH // PROJ_COLS)]
    sinks_b = [(base_a + k * ATTN_WIDTH + col, pair_sink(ref, col))
               for k, ref in enumerate((qa_ref, ka_ref, va_ref, za_ref))
               for col in range(0, ATTN_WIDTH, PROJ_COLS)]
    sinks_b += [(base_x + n * PROJ_COLS, px_sink(n * PROJ_COLS))
                for n in range(2 * XATTN_WIDTH // PROJ_COLS)]

    first, second = pl.ds(0, L), pl.ds(L, L)
    zm_cols, om_cols = pl.ds(0, MLSTM_WIDTH), pl.ds(MLSTM_WIDTH, MLSTM_WIDTH)
    _round_robin(
        _inproj_pieces(h_scr, w_ref, sinks_a),
        _mlstm_front(xm_prev, *front_args, stage1),
        _mlstm_back(stage0, zo_prev.at[first, zm_cols], zo_prev.at[first, om_cols], *back_args,
                    ym_ref.at[first], cn_scr, m_scr))

    @pl.when(j % steps_per_seq == 0)
    def _():
        xbuf[0:8, :] = jnp.zeros((8, MLSTM_WIDTH), F32)

    _round_robin(
        _inproj_pieces(h_scr, w_ref, sinks_b),
        _mlstm_front(pm_cur.at[first, pl.ds(0, MLSTM_WIDTH)], *front_args, stage0),
        _mlstm_back(stage1, zo_prev.at[second, zm_cols], zo_prev.at[second, om_cols], *back_args,
                    ym_ref.at[second], cn_scr, m_scr))

    xm_prev[...] = pm_cur[L:2 * L, 0:MLSTM_WIDTH]
    zo_prev[...] = pm_cur[:, MLSTM_WIDTH:3 * MLSTM_WIDTH]


def _inproj_mlstm(x2, g_norm, w_in_b, conv_w, conv_b, wq_d, wk_d, wv_d, wg_pad, bg_pad, g_head, skip,
                  batch):
    L = MLSTM_CHUNK
    tile = 2 * L
    tokens = x2.shape[0]
    tiles = tokens // tile
    tiles_per_seq = SEQ // tile
    assert SEQ % tile == 0
    dh = MLSTM_HEAD_DIM
    cur = lambda j: jnp.minimum(j, tiles - 1)
    full = lambda shape: pl.BlockSpec(shape, lambda j: (0,) * len(shape))
    pair_spec = pl.BlockSpec((None, ATTN_PAIRS, tile, LANES),
                             lambda j: (cur(j) // tiles_per_seq, 0, cur(j) % tiles_per_seq, 0))
    pair_shape = jax.ShapeDtypeStruct((batch, ATTN_PAIRS, SEQ, LANES), BF16)
    stage = [pltpu.VMEM((L, MLSTM_WIDTH), F32), pltpu.VMEM((L, MLSTM_WIDTH), BF16),
             pltpu.VMEM((L, MLSTM_WIDTH), F32), pltpu.VMEM((L, MLSTM_WIDTH), BF16),
             pltpu.VMEM((L, LANES), F32), pltpu.VMEM((LANES, L), F32)]
    return pl.pallas_call(
        _inproj_mlstm_kernel,
        grid=(tiles + 1,),
        in_specs=[
            pl.BlockSpec((tile, D_MODEL), lambda j: (cur(j), 0)),
            full((1, D_MODEL)),
            pl.BlockSpec((D_MODEL, IN_WIDTH), lambda j: (0, 0), pipeline_mode=pl.Buffered(1)),
            full((CONV_WIDTH, MLSTM_WIDTH)), full((1, MLSTM_WIDTH)),
            full((MLSTM_HEADS, dh, dh)), full((MLSTM_HEADS, dh, dh)), full((MLSTM_HEADS, dh, dh)),
            full((3 * MLSTM_WIDTH, LANES)), full((1, LANES)),
            full((1, MLSTM_WIDTH)), full((1, MLSTM_WIDTH)),
        ],
        out_specs=[
            pair_spec, pair_spec, pair_spec, pair_spec,
            pl.BlockSpec((tile, 2 * XATTN_WIDTH), lambda j: (cur(j), 0)),
            pl.BlockSpec((tile, MLSTM_WIDTH), lambda j: (jnp.maximum(j - 1, 0), 0)),
        ],
        out_shape=[
            pair_shape, pair_shape, pair_shape, pair_shape,
            jax.ShapeDtypeStruct((tokens, 2 * XATTN_WIDTH), BF16),
            jax.ShapeDtypeStruct((tokens, MLSTM_WIDTH), BF16),
        ],
        scratch_shapes=[
            pltpu.VMEM((tile, D_MODEL), BF16),
            pltpu.VMEM((tile, 3 * MLSTM_WIDTH), BF16),
            pltpu.VMEM((L, MLSTM_WIDTH), BF16),
            pltpu.VMEM((tile, 2 * MLSTM_WIDTH), BF16),
            pltpu.VMEM((L + 8, MLSTM_WIDTH), F32),
            pltpu.VMEM((MLSTM_HEADS, dh, dh + LANES), F32),
            pltpu.VMEM((MLSTM_HEADS, 8, LANES), F32),
        ] + stage + stage,
        compiler_params=pltpu.CompilerParams(
            dimension_semantics=("arbitrary",), vmem_limit_bytes=VMEM_LIMIT_BYTES),
        name="in_projection_mlstm",
    )(x2, g_norm, w_in_b, conv_w, conv_b, wq_d, wk_d, wv_d, wg_pad, bg_pad, g_head, skip)


def _band_scores(q_top, q_bot, kk, bias):
    q2 = jnp.concatenate([q_top, q_bot], axis=0).astype(BF16)
    return _dot_nt(q2, kk.astype(BF16)) + jnp.concatenate([bias, bias], axis=0)


def _band_output(s, vv):
    nk = vv.shape[0]
    m = jnp.max(s, axis=1, keepdims=True)
    p = jnp.exp2(s - m).astype(BF16)
    v_aug = jnp.concatenate([vv.astype(BF16), jnp.ones((nk, LANES), BF16)], axis=1)
    o2 = _dot(p, v_aug)
    first = lax.broadcasted_iota(jnp.int32, (BAND, LANES), 1) < ATTN_HEAD_DIM
    o = jnp.where(first, o2[:BAND, :LANES], o2[BAND:, :LANES])
    l_p = jnp.where(first, o2[:BAND, LANES:], o2[BAND:, LANES:])
    m_p = jnp.where(first, m[:BAND], m[BAND:])
    return o, m_p, l_p


def _merge(a, b):
    (o_a, m_a, l_a), (o_b, m_b, l_b) = a, b
    m_new = jnp.maximum(m_a, m_b)
    w_a = jnp.exp2(m_a - m_new)
    w_b = jnp.exp2(m_b - m_new)
    return o_a * w_a + o_b * w_b, m_new, l_a * w_a + l_b * w_b


def _dilated_kernel(q_ref, k_ref, v_ref, z_ref, ra_ref, rb_ref, rc_ref, out_ref,
                    q0, q1, ks, vs, q40, q41, k4, v4, a1o, a1m, a1l, a4o, a4m, a4l, zbuf,
                    bias_prev, bias_self):
    nblk = SEQ // BAND
    span = DIL_MID * BAND
    q_scale = ATTN_HEAD_DIM ** -0.5 * LOG2E

    t = lax.broadcasted_iota(jnp.int32, (BAND, 2 * BAND), 0)
    c = lax.broadcasted_iota(jnp.int32, (BAND, 2 * BAND), 1)
    bias_prev[...] = jnp.where((c >= t) & (c <= t + BAND), 0.0, NEG)
    t = lax.broadcasted_iota(jnp.int32, (BAND, BAND), 0)
    c = lax.broadcasted_iota(jnp.int32, (BAND, BAND), 1)
    bias_self[...] = jnp.where(c <= t, 0.0, NEG)

    def rope(x, rows):
        return (x * ra_ref[rows, :] + pltpu.roll(x, LANES - ROPE_DIM // 2, 1) * rb_ref[rows, :]
                + pltpu.roll(x, ROPE_DIM // 2, 1) * rc_ref[rows, :])

    def prep(i, carry):
        rows = pl.ds(pl.multiple_of(i * BAND, BAND), BAND)
        q = rope(q_ref[rows, :].astype(F32), rows) * q_scale
        first = lax.broadcasted_iota(jnp.int32, (BAND, LANES), 1) < ATTN_HEAD_DIM
        q0[rows, :] = jnp.where(first, q, 0.0)
        q1[rows, :] = jnp.where(first, 0.0, q)
        ks[rows, :] = rope(k_ref[rows, :].astype(F32), rows)
        vs[rows, :] = v_ref[rows, :].astype(F32)
        zbuf[rows, :] = _silu(z_ref[rows, :].astype(F32))
        return carry

    lax.fori_loop(0, nblk, prep, 0, unroll=4)

    def deinterleave(idx, carry):
        src = pl.ds(idx // DIL_MID + (idx % DIL_MID) * span, BAND, stride=DIL_MID)
        dst = pl.ds(pl.multiple_of(idx * BAND, BAND), BAND)
        q40[dst, :] = q0[src, :]
        q41[dst, :] = q1[src, :]
        k4[dst, :] = ks[src, :]
        v4[dst, :] = vs[src, :]
        return carry

    lax.fori_loop(0, nblk, deinterleave, 0, unroll=4)

    def put(acc, rows, res):
        for ref, val in zip(acc, res):
            ref[rows, :] = val

    acc1 = (a1o, a1m, a1l)
    acc4 = (a4o, a4m, a4l)

    tasks = []
    for qa, qb, kbuf, vbuf, acc, per_class in ((q0, q1, ks, vs, acc1, nblk),
                                                (q40, q41, k4, v4, acc4, nblk // DIL_MID)):
        for b in range(nblk):
            rows = pl.ds(b * BAND, BAND)
            win = None if b % per_class == 0 else pl.ds((b - 1) * BAND, 2 * BAND)
            tasks.append((qa, qb, kbuf, vbuf, rows, win, acc, False))
    for cls in range(DIL_MID):
        for r in range(DIL_MID):
            rows = pl.ds(cls * (SEQ // DIL_MID) + r, BAND, stride=DIL_MID)
            tasks.append((q40, q41, k4, v4, rows, None, acc4, True))

    def scores(task):
        qa, qb, kbuf, _, rows, win, _, _ = task
        if win is None:
            return _band_scores(qa[rows, :], qb[rows, :], kbuf[rows, :], bias_self[...])
        return _band_scores(qa[rows, :], qb[rows, :], kbuf[win, :], bias_prev[...])

    def complete(task, s):
        _, _, _, vbuf, rows, win, acc, merge = task
        res = _band_output(s, vbuf[rows if win is None else win, :])
        if merge:
            res = _merge(tuple(ref[rows, :] for ref in acc), res)
        put(acc, rows, res)

    lookahead = 4
    pending = []
    for task in tasks:
        pending.append((task, scores(task)))
        if len(pending) > lookahead:
            complete(*pending.pop(0))
    for item in pending:
        complete(*item)

    for b in range(nblk):
        rows4 = pl.ds(b * BAND, BAND)
        rows1 = pl.ds(b // DIL_MID + (b % DIL_MID) * span, BAND, stride=DIL_MID)
        o, _, l = _merge(tuple(ref[rows1, :] for ref in acc1), tuple(ref[rows4, :] for ref in acc4))
        zbuf[rows1, :] = o * (1.0 / l) * zbuf[rows1, :]

    def finish(i, carry):
        rows = pl.ds(pl.multiple_of(i * BAND, BAND), BAND)
        out_ref[rows, :] = zbuf[rows, :].astype(BF16)
        return carry

    lax.fori_loop(0, nblk, finish, 0, unroll=4)


def _dilated_attention(qa, ka, va, za, rope_a, rope_b, rope_c, batch):
    slab = pl.BlockSpec((None, None, SEQ, LANES), lambda b, j: (b, j, 0, 0))
    table = pl.BlockSpec((SEQ, LANES), lambda b, j: (0, 0))
    return pl.pallas_call(
        _dilated_kernel,
        grid=(batch, ATTN_PAIRS),
        in_specs=[slab, slab, slab, slab, table, table, table],
        out_specs=slab,
        out_shape=jax.ShapeDtypeStruct((batch, ATTN_PAIRS, SEQ, LANES), BF16),
        scratch_shapes=[pltpu.VMEM((SEQ, LANES), F32) for _ in range(15)] + [
            pltpu.VMEM((BAND, 2 * BAND), F32), pltpu.VMEM((BAND, BAND), F32)],
        compiler_params=pltpu.CompilerParams(
            dimension_semantics=("arbitrary", "arbitrary"), vmem_limit_bytes=VMEM_LIMIT_BYTES),
        name="dilated_attention",
    )(qa, ka, va, za, rope_a, rope_b, rope_c)


def _outproj_kernel(x_ref, ym_ref, ya_ref, px_ref, mem_ref, gmem_ref, wkv_ref, wout_ref,
                    gfin_ref, out_ref, kx_scr, vx_scr):
    @pl.when(pl.program_id(1) == 0)
    def _():
        mem = mem_ref[...]
        ms = jnp.mean(mem * mem, axis=-1, keepdims=True)
        mem_n = (mem * lax.rsqrt(ms + EPS) * gmem_ref[...]).astype(BF16)
        kv = _dot(mem_n, wkv_ref[...])
        kx_scr[...] = kv[:, :XATTN_WIDTH].astype(BF16)
        vx_scr[...] = kv[:, XATTN_WIDTH:].astype(BF16)

    scale = XATTN_HEAD_DIM ** -0.5 * LOG2E
    ones_b = jnp.ones((N_MEM, LANES), BF16)
    y_a = jnp.concatenate([ya_ref[j] for j in range(ATTN_PAIRS)], axis=1)
    y = _dot(y_a, wout_ref[MLSTM_WIDTH:MLSTM_WIDTH + ATTN_WIDTH, :])
    heads = []
    for h in range(XATTN_HEADS):
        sl = slice(h * XATTN_HEAD_DIM, (h + 1) * XATTN_HEAD_DIM)
        s = _dot_nt(px_ref[:, sl], kx_scr[:, sl]) * scale
        m = jnp.max(s, axis=1, keepdims=True)
        p = jnp.exp2(s - m).astype(BF16)
        o2 = _dot(p, jnp.concatenate([vx_scr[:, sl], ones_b], axis=1))
        heads.append(o2[:, :LANES] * (1.0 / o2[:, LANES:]))
        rows_m = slice(h * MLSTM_HEAD_DIM, (h + 1) * MLSTM_HEAD_DIM)
        y = y + _dot(ym_ref[:, rows_m], wout_ref[rows_m, :])
    o_x = jnp.concatenate(heads, axis=1)
    y_x = (o_x * _silu(px_ref[:, XATTN_WIDTH:].astype(F32))).astype(BF16)
    y = y + _dot(y_x, wout_ref[MLSTM_WIDTH + ATTN_WIDTH:, :])
    r = x_ref[...] + y
    ms = jnp.mean(r * r, axis=-1, keepdims=True)
    out_ref[...] = r * lax.rsqrt(ms + EPS) * gfin_ref[...]


def _output_projection(x2, y_m, y_a, px, mem, g_mem, wkv_b, wout_b, g_final, batch):
    tokens = x2.shape[0]
    tps = SEQ // ROW_TILE
    rows = lambda width: pl.BlockSpec((ROW_TILE, width), lambda b, i: (b * tps + i, 0))
    full = lambda shape: pl.BlockSpec(shape, lambda b, i: (0,) * len(shape))
    return pl.pallas_call(
        _outproj_kernel,
        grid=(batch, tps),
        in_specs=[
            rows(D_MODEL), rows(MLSTM_WIDTH),
            pl.BlockSpec((None, ATTN_PAIRS, ROW_TILE, LANES), lambda b, i: (b, 0, i, 0)),
            rows(2 * XATTN_WIDTH),
            pl.BlockSpec((None, N_MEM, D_MODEL), lambda b, i: (b, 0, 0)),
            full((1, D_MODEL)), full((D_MODEL, 2 * XATTN_WIDTH)), full((MIX_WIDTH, D_MODEL)),
            full((1, D_MODEL)),
        ],
        out_specs=rows(D_MODEL),
        out_shape=jax.ShapeDtypeStruct((tokens, D_MODEL), F32),
        scratch_shapes=[pltpu.VMEM((N_MEM, XATTN_WIDTH), BF16), pltpu.VMEM((N_MEM, XATTN_WIDTH), BF16)],
        compiler_params=pltpu.CompilerParams(
            dimension_semantics=("arbitrary", "arbitrary"), vmem_limit_bytes=VMEM_LIMIT_BYTES),
        name="out_projection",
    )(x2, y_m, y_a, px, mem, g_mem, wkv_b, wout_b, g_final)


def _block_diag_dense(w_blk):
    rows = w_blk.reshape(MLSTM_HEADS, MLSTM_HEAD_DIM, QKV_BLOCK)
    tiled = jnp.tile(rows, (1, 1, MLSTM_HEAD_DIM // QKV_BLOCK))
    idx = jnp.arange(MLSTM_HEAD_DIM) // QKV_BLOCK
    same_block = idx[:, None] == idx[None, :]
    return jnp.where(same_block[None], tiled, 0.0).astype(BF16)


def _rope_tables():
    half = ROPE_DIM // 2
    pos = jnp.arange(SEQ, dtype=F32)
    inv = ROPE_THETA ** (-jnp.arange(0, ROPE_DIM, 2, dtype=F32) / ROPE_DIM)
    ang = pos[:, None] * inv[None, :]
    cos, sin = jnp.cos(ang), jnp.sin(ang)
    ones = jnp.ones((SEQ, ATTN_HEAD_DIM - ROPE_DIM), F32)
    zeros = jnp.zeros((SEQ, ATTN_HEAD_DIM - ROPE_DIM), F32)
    zero_h = jnp.zeros((SEQ, half), F32)
    a = jnp.concatenate([cos, cos, ones], axis=1)
    b = jnp.concatenate([-sin, zero_h, zeros], axis=1)
    c = jnp.concatenate([zero_h, sin, zeros], axis=1)
    tile = lambda t: jnp.concatenate([t, t], axis=1)
    return tile(a), tile(b), tile(c)


def kernel(x, mem, g_norm, w_in, conv_w, conv_b, w_q_blk, w_k_blk, w_v_blk, w_gate, b_gate,
           g_head, skip, g_mem, w_mem_kv, w_out, g_final):
    batch = x.shape[0]
    assert x.shape[1:] == (SEQ, D_MODEL) and g_norm.shape[0] == 1
    layer = 0
    x2 = x.reshape(batch * SEQ, D_MODEL)
    wg_pad = jnp.pad(w_gate[layer], ((0, 0), (0, LANES - 2 * MLSTM_HEADS))).astype(BF16)
    bg_pad = jnp.pad(b_gate[layer], (0, LANES - 2 * MLSTM_HEADS))[None, :]
    qa, ka, va, za, px, y_m = _inproj_mlstm(
        x2, g_norm[layer][None, :], w_in[layer].astype(BF16), conv_w[layer], conv_b[layer][None, :],
        _block_diag_dense(w_q_blk[layer]), _block_diag_dense(w_k_blk[layer]),
        _block_diag_dense(w_v_blk[layer]), wg_pad, bg_pad,
        g_head[layer][None, :], skip[layer][None, :], batch)

    rope_a, rope_b, rope_c = _rope_tables()
    y_a = _dilated_attention(qa, ka, va, za, rope_a, rope_b, rope_c, batch)

    out = _output_projection(x2, y_m, y_a, px, mem,
                             g_mem[layer][None, :], w_mem_kv[layer].astype(BF16),
                             w_out[layer].astype(BF16), g_final[None, :], batch)
    return out.reshape(batch, SEQ, D_MODEL)
```

```python
import jax
import jax.numpy as jnp
from jax import lax
from jax.experimental import pallas as pl
from jax.experimental.pallas import tpu as pltpu

F32 = jnp.float32
BF16 = jnp.bfloat16

D_MODEL = 1024
SEQ = 2048
N_MEM = 256
EPS = 1e-6

MLSTM_HEADS = 4
MLSTM_HEAD_DIM = 256
MLSTM_WIDTH = 1024
QKV_BLOCK = 4
CONV_WIDTH = 4

ATTN_HEADS = 8
ATTN_HEAD_DIM = 64
ATTN_WIDTH = 512
ATTN_PAIRS = ATTN_HEADS // 2
ROPE_DIM = 16
ROPE_THETA = 500000.0
BAND = 128
DIL_MID = 4
DIL_MAX = 16

XATTN_HEADS = 4
XATTN_HEAD_DIM = 128
XATTN_WIDTH = 512

MIX_WIDTH = MLSTM_WIDTH + ATTN_WIDTH + XATTN_WIDTH
IN_WIDTH = 3 * MLSTM_WIDTH + 4 * ATTN_WIDTH + 2 * XATTN_WIDTH

LANES = 128
VMEM_LIMIT_BYTES = 56 * 1024 * 1024
NEG = -1e30
LOG2E = 1.4426950408889634

ROW_TILE = 512
PROJ_COLS = 512
MLSTM_CHUNK = 256


def _sigmoid(x):
    return 1.0 / (1.0 + jnp.exp2(x * (-LOG2E)))


def _silu(x):
    return x * _sigmoid(x)


def _log_sigmoid(x):
    return jnp.minimum(x, 0.0) - jnp.log(1.0 + jnp.exp(-jnp.abs(x)))


def _dot(a, b):
    return jnp.dot(a, b, preferred_element_type=F32)


def _dot_nt(a, b):
    return lax.dot_general(a, b, (((1,), (1,)), ((), ())), preferred_element_type=F32)


def _dot_tn(a, b):
    return lax.dot_general(a, b, (((0,), (0,)), ((), ())), preferred_element_type=F32)


def _mlstm_front(xm_ref, convw_ref, convb_ref, wq_ref, wk_ref, wv_ref, wg_ref, bg_ref, xbuf, stage):
    L = MLSTM_CHUNK
    dh = MLSTM_HEAD_DIM
    xc_ref, q_ref, k_ref, v_ref, gcol_ref, grow_ref = stage

    xbuf[8:8 + L, :] = xm_ref[...].astype(F32)
    conv = convb_ref[...] + convw_ref[3:4, :] * xbuf[8:8 + L, :]
    conv = conv + convw_ref[2:3, :] * xbuf[7:7 + L, :]
    conv = conv + convw_ref[1:2, :] * xbuf[6:6 + L, :]
    conv = conv + convw_ref[0:1, :] * xbuf[5:5 + L, :]
    xbuf[0:8, :] = xbuf[L:L + 8, :]
    xc = _silu(conv)
    xc_ref[...] = xc
    xc_b = xc.astype(BF16)
    yield

    k_scale = dh ** -0.5
    q_b, k_b, v_b = [], [], []
    for h in range(MLSTM_HEADS):
        sl = slice(h * dh, (h + 1) * dh)
        q_b.append(_dot(xc_b[:, sl], wq_ref[h]).astype(BF16))
        k = _dot(xc_b[:, sl], wk_ref[h])
        k_ref[:, sl] = k * k_scale
        k_b.append(k.astype(BF16))
        v_b.append(_dot(xm_ref[:, sl], wv_ref[h]).astype(BF16))
        q_ref[:, sl] = q_b[h]
        v_ref[:, sl] = v_b[h]
        yield

    qkv_b = jnp.concatenate(q_b + k_b + v_b, axis=1)
    gates = _dot(qkv_b, wg_ref[...]) + bg_ref[...]
    causal = lax.broadcasted_iota(jnp.int32, (L, L), 0) >= lax.broadcasted_iota(jnp.int32, (L, L), 1)
    lf = _log_sigmoid(gates) * LOG2E
    hi = lf.astype(BF16)
    rest = lf - hi.astype(F32)
    mid = rest.astype(BF16)
    lo = (rest - mid.astype(F32)).astype(BF16)
    parts = _dot(causal.astype(BF16), jnp.concatenate([hi, mid, lo], axis=1))
    b2 = parts[:, :LANES] + parts[:, LANES:2 * LANES] + parts[:, 2 * LANES:]
    w2 = gates * LOG2E - pltpu.roll(b2, LANES - MLSTM_HEADS, 1)
    lane = lax.broadcasted_iota(jnp.int32, (L, LANES), 1)
    gcol = jnp.where(lane < MLSTM_HEADS, w2, b2)
    gcol_ref[...] = gcol
    grow_ref[...] = gcol.T


def _mlstm_back(stage, zm_ref, om_ref, ghead_ref, skip_ref, y_ref, c_scr, n_scr, m_scr):
    L = MLSTM_CHUNK
    dh = MLSTM_HEAD_DIM
    xc_ref, q_ref, k_ref, v_ref, gcol_ref, grow_ref = stage
    causal = lax.broadcasted_iota(jnp.int32, (L, L), 0) >= lax.broadcasted_iota(jnp.int32, (L, L), 1)

    for h in range(MLSTM_HEADS):
        sl = slice(h * dh, (h + 1) * dh)
        w_col = gcol_ref[:, h:h + 1]
        b_col = gcol_ref[:, MLSTM_HEADS + h:MLSTM_HEADS + h + 1]
        w_row = grow_ref[h:h + 1, :]
        m_prev = m_scr[h, 0:1, 0:1]
        c_old = c_scr[h]
        n_old = n_scr[h, 0:1, :]
        q_b = q_ref[:, sl]
        k_f = k_ref[:, sl]
        v_b = v_ref[:, sl]

        mw = jnp.where(causal, w_row, NEG)
        mm = jnp.maximum(m_prev, jnp.max(mw, axis=1, keepdims=True))
        wts = jnp.exp2(mw - mm) * _dot_nt(q_b, k_f.astype(BF16))
        inter = jnp.exp2(m_prev - mm)
        den = inter * jnp.sum(q_b.astype(F32) * n_old, axis=1, keepdims=True) + \
            jnp.sum(wts, axis=1, keepdims=True)
        r = 1.0 / jnp.maximum(jnp.abs(den), jnp.exp2(-(b_col + mm)))
        h_t = (inter * _dot(q_b, c_old.astype(BF16)) + _dot(wts.astype(BF16), v_b)) * r

        mm_last = mm[L - 1:L, :]
        gk = jnp.exp2(w_col - mm_last) * k_f
        decay = jnp.exp2(m_prev - mm_last)
        c_scr[h] = decay * c_old + _dot_tn(gk.astype(BF16), v_b)
        n_scr[h, 0:1, :] = decay * n_old + jnp.sum(gk, axis=0, keepdims=True)
        m_scr[h] = jnp.broadcast_to(b_col[L - 1:L, :] + mm_last, (8, LANES))
        yield

        hg = _sigmoid(om_ref[:, sl].astype(F32)) * h_t
        mu = jnp.mean(hg, axis=1, keepdims=True)
        cen = hg - mu
        var = jnp.mean(cen * cen, axis=1, keepdims=True)
        ln = cen * lax.rsqrt(var + EPS) * ghead_ref[:, sl]
        y = (ln + skip_ref[:, sl] * xc_ref[:, sl]) * _silu(zm_ref[:, sl].astype(F32))
        y_ref[:, sl] = y.astype(BF16)
        yield


def _round_robin(*gens):
    done = object()
    live = list(gens)
    while live:
        live = [g for g in live if next(g, done) is not done]


def _inproj_pieces(h_scr, w_ref, sinks):
    for col, store in sinks:
        store(_dot(h_scr[...], w_ref[:, col:col + PROJ_COLS]).astype(BF16))
        yield


def _inproj_mlstm_kernel(x_ref, g_ref, w_ref, convw_ref, convb_ref, wq_ref, wk_ref, wv_ref,
                         wg_ref, bg_ref, ghead_ref, skip_ref,
                         qa_ref, ka_ref, va_ref, za_ref, px_ref, ym_ref,
                         h_scr, pm_cur, xm_prev, zo_prev, xbuf, c_scr, n_scr, m_scr, *stages):
    L = MLSTM_CHUNK
    steps_per_seq = SEQ // (2 * L)
    j = pl.program_id(0)
    stage0, stage1 = stages[:len(stages) // 2], stages[len(stages) // 2:]
    front_args = (convw_ref, convb_ref, wq_ref, wk_ref, wv_ref, wg_ref, bg_ref, xbuf)
    back_args = (ghead_ref, skip_ref)

    @pl.when(j == 0)
    def _():
        xbuf[0:8, :] = jnp.zeros((8, MLSTM_WIDTH), F32)
        xm_prev[...] = jnp.zeros_like(xm_prev)
        zo_prev[...] = jnp.zeros_like(zo_prev)
        for ref in stage0:
            ref[...] = jnp.zeros_like(ref)

    @pl.when((j == 0) | (j % steps_per_seq == 1))
    def _():
        c_scr[...] = jnp.zeros_like(c_scr)
        n_scr[...] = jnp.zeros_like(n_scr)
        m_scr[...] = jnp.zeros_like(m_scr)

    x = x_ref[...]
    ms = jnp.mean(x * x, axis=-1, keepdims=True)
    h_scr[...] = (x * lax.rsqrt(ms + EPS) * g_ref[...]).astype(BF16)

    def pm_sink(col):
        def store(r):
            pm_cur[:, col:col + PROJ_COLS] = r
        return store

    def pair_sink(ref, col):
        def store(r):
            for p in range(PROJ_COLS // LANES):
                ref[col // LANES + p] = r[:, p * LANES:(p + 1) * LANES]
        return store

    def px_sink(col):
        def store(r):
            px_ref[:, col:col + PROJ_COLS] = r
        return store

    base_a = 3 * MLSTM_WIDTH
    base_x = base_a + 4 * ATTN_WIDTH
    sinks_a = [(n * PROJ_COLS, pm_sink(n * PROJ_COLS)) for n in range(3 * MLSTM_WIDTH // PROJ_COLS)]
    sinks_b = [(base_a + k * ATTN_WIDTH + col, pair_sink(ref, col))
               for k, ref in enumerate((qa_ref, ka_ref, va_ref, za_ref))
               for col in range(0, ATTN_WIDTH, PROJ_COLS)]
    sinks_b += [(base_x + n * PROJ_COLS, px_sink(n * PROJ_COLS))
                for n in range(2 * XATTN_WIDTH // PROJ_COLS)]

    first, second = pl.ds(0, L), pl.ds(L, L)
    zm_cols, om_cols = pl.ds(0, MLSTM_WIDTH), pl.ds(MLSTM_WIDTH, MLSTM_WIDTH)
    _round_robin(
        _inproj_pieces(h_scr, w_ref, sinks_a),
        _mlstm_front(xm_prev, *front_args, stage1),
        _mlstm_back(stage0, zo_prev.at[first, zm_cols], zo_prev.at[first, om_cols], *back_args,
                    ym_ref.at[first], c_scr, n_scr, m_scr))

    @pl.when(j % steps_per_seq == 0)
    def _():
        xbuf[0:8, :] = jnp.zeros((8, MLSTM_WIDTH), F32)

    _round_robin(
        _inproj_pieces(h_scr, w_ref, sinks_b),
        _mlstm_front(pm_cur.at[first, pl.ds(0, MLSTM_WIDTH)], *front_args, stage0),
        _mlstm_back(stage1, zo_prev.at[second, zm_cols], zo_prev.at[second, om_cols], *back_args,
                    ym_ref.at[second], c_scr, n_scr, m_scr))

    xm_prev[...] = pm_cur[L:2 * L, 0:MLSTM_WIDTH]
    zo_prev[...] = pm_cur[:, MLSTM_WIDTH:3 * MLSTM_WIDTH]


def _inproj_mlstm(x2, g_norm, w_in_b, conv_w, conv_b, wq_d, wk_d, wv_d, wg_pad, bg_pad, g_head, skip,
                  batch):
    L = MLSTM_CHUNK
    tile = 2 * L
    tokens = x2.shape[0]
    tiles = tokens // tile
    tiles_per_seq = SEQ // tile
    assert SEQ % tile == 0
    dh = MLSTM_HEAD_DIM
    cur = lambda j: jnp.minimum(j, tiles - 1)
    full = lambda shape: pl.BlockSpec(shape, lambda j: (0,) * len(shape))
    pair_spec = pl.BlockSpec((None, ATTN_PAIRS, tile, LANES),
                             lambda j: (cur(j) // tiles_per_seq, 0, cur(j) % tiles_per_seq, 0))
    pair_shape = jax.ShapeDtypeStruct((batch, ATTN_PAIRS, SEQ, LANES), BF16)
    stage = [pltpu.VMEM((L, MLSTM_WIDTH), F32), pltpu.VMEM((L, MLSTM_WIDTH), BF16),
             pltpu.VMEM((L, MLSTM_WIDTH), F32), pltpu.VMEM((L, MLSTM_WIDTH), BF16),
             pltpu.VMEM((L, LANES), F32), pltpu.VMEM((LANES, L), F32)]
    return pl.pallas_call(
        _inproj_mlstm_kernel,
        grid=(tiles + 1,),
        in_specs=[
            pl.BlockSpec((tile, D_MODEL), lambda j: (cur(j), 0)),
            full((1, D_MODEL)),
            pl.BlockSpec((D_MODEL, IN_WIDTH), lambda j: (0, 0), pipeline_mode=pl.Buffered(1)),
            full((CONV_WIDTH, MLSTM_WIDTH)), full((1, MLSTM_WIDTH)),
            full((MLSTM_HEADS, dh, dh)), full((MLSTM_HEADS, dh, dh)), full((MLSTM_HEADS, dh, dh)),
            full((3 * MLSTM_WIDTH, LANES)), full((1, LANES)),
            full((1, MLSTM_WIDTH)), full((1, MLSTM_WIDTH)),
        ],
        out_specs=[
            pair_spec, pair_spec, pair_spec, pair_spec,
            pl.BlockSpec((tile, 2 * XATTN_WIDTH), lambda j: (cur(j), 0)),
            pl.BlockSpec((tile, MLSTM_WIDTH), lambda j: (jnp.maximum(j - 1, 0), 0)),
        ],
        out_shape=[
            pair_shape, pair_shape, pair_shape, pair_shape,
            jax.ShapeDtypeStruct((tokens, 2 * XATTN_WIDTH), BF16),
            jax.ShapeDtypeStruct((tokens, MLSTM_WIDTH), BF16),
        ],
        scratch_shapes=[
            pltpu.VMEM((tile, D_MODEL), BF16),
            pltpu.VMEM((tile, 3 * MLSTM_WIDTH), BF16),
            pltpu.VMEM((L, MLSTM_WIDTH), BF16),
            pltpu.VMEM((tile, 2 * MLSTM_WIDTH), BF16),
            pltpu.VMEM((L + 8, MLSTM_WIDTH), F32),
            pltpu.VMEM((MLSTM_HEADS, dh, dh), F32),
            pltpu.VMEM((MLSTM_HEADS, 8, dh), F32),
            pltpu.VMEM((MLSTM_HEADS, 8, LANES), F32),
        ] + stage + stage,
        compiler_params=pltpu.CompilerParams(
            dimension_semantics=("arbitrary",), vmem_limit_bytes=VMEM_LIMIT_BYTES),
        name="in_projection_mlstm",
    )(x2, g_norm, w_in_b, conv_w, conv_b, wq_d, wk_d, wv_d, wg_pad, bg_pad, g_head, skip)


def _band_scores(q_top, q_bot, kk, bias):
    q2 = jnp.concatenate([q_top, q_bot], axis=0).astype(BF16)
    return _dot_nt(q2, kk.astype(BF16)) + jnp.concatenate([bias, bias], axis=0)


def _band_output(s, vv):
    nk = vv.shape[0]
    m = jnp.max(s, axis=1, keepdims=True)
    p = jnp.exp2(s - m).astype(BF16)
    v_aug = jnp.concatenate([vv.astype(BF16), jnp.ones((nk, LANES), BF16)], axis=1)
    o2 = _dot(p, v_aug)
    first = lax.broadcasted_iota(jnp.int32, (BAND, LANES), 1) < ATTN_HEAD_DIM
    o = jnp.where(first, o2[:BAND, :LANES], o2[BAND:, :LANES])
    l_p = jnp.where(first, o2[:BAND, LANES:], o2[BAND:, LANES:])
    m_p = jnp.where(first, m[:BAND], m[BAND:])
    return o, m_p, l_p


def _merge(a, b):
    (o_a, m_a, l_a), (o_b, m_b, l_b) = a, b
    m_new = jnp.maximum(m_a, m_b)
    w_a = jnp.exp2(m_a - m_new)
    w_b = jnp.exp2(m_b - m_new)
    return o_a * w_a + o_b * w_b, m_new, l_a * w_a + l_b * w_b


def _dilated_kernel(q_ref, k_ref, v_ref, z_ref, ra_ref, rb_ref, rc_ref, out_ref,
                    q0, q1, ks, vs, q40, q41, k4, v4, a1o, a1m, a1l, a4o, a4m, a4l, zbuf,
                    bias_prev, bias_self):
    nblk = SEQ // BAND
    span = DIL_MID * BAND
    q_scale = ATTN_HEAD_DIM ** -0.5 * LOG2E

    t = lax.broadcasted_iota(jnp.int32, (BAND, 2 * BAND), 0)
    c = lax.broadcasted_iota(jnp.int32, (BAND, 2 * BAND), 1)
    bias_prev[...] = jnp.where((c >= t) & (c <= t + BAND), 0.0, NEG)
    t = lax.broadcasted_iota(jnp.int32, (BAND, BAND), 0)
    c = lax.broadcasted_iota(jnp.int32, (BAND, BAND), 1)
    bias_self[...] = jnp.where(c <= t, 0.0, NEG)

    def rope(x, rows):
        return (x * ra_ref[rows, :] + pltpu.roll(x, LANES - ROPE_DIM // 2, 1) * rb_ref[rows, :]
                + pltpu.roll(x, ROPE_DIM // 2, 1) * rc_ref[rows, :])

    def prep(i, carry):
        rows = pl.ds(pl.multiple_of(i * BAND, BAND), BAND)
        q = rope(q_ref[rows, :].astype(F32), rows) * q_scale
        first = lax.broadcasted_iota(jnp.int32, (BAND, LANES), 1) < ATTN_HEAD_DIM
        q0[rows, :] = jnp.where(first, q, 0.0)
        q1[rows, :] = jnp.where(first, 0.0, q)
        ks[rows, :] = rope(k_ref[rows, :].astype(F32), rows)
        vs[rows, :] = v_ref[rows, :].astype(F32)
        zbuf[rows, :] = _silu(z_ref[rows, :].astype(F32))
        return carry

    lax.fori_loop(0, nblk, prep, 0, unroll=4)

    def deinterleave(idx, carry):
        src = pl.ds(idx // DIL_MID + (idx % DIL_MID) * span, BAND, stride=DIL_MID)
        dst = pl.ds(pl.multiple_of(idx * BAND, BAND), BAND)
        q40[dst, :] = q0[src, :]
        q41[dst, :] = q1[src, :]
        k4[dst, :] = ks[src, :]
        v4[dst, :] = vs[src, :]
        return carry

    lax.fori_loop(0, nblk, deinterleave, 0, unroll=4)

    def put(acc, rows, res):
        for ref, val in zip(acc, res):
            ref[rows, :] = val

    acc1 = (a1o, a1m, a1l)
    acc4 = (a4o, a4m, a4l)

    tasks = []
    for qa, qb, kbuf, vbuf, acc, per_class in ((q0, q1, ks, vs, acc1, nblk),
                                                (q40, q41, k4, v4, acc4, nblk // DIL_MID)):
        for b in range(nblk):
            rows = pl.ds(b * BAND, BAND)
            win = None if b % per_class == 0 else pl.ds((b - 1) * BAND, 2 * BAND)
            tasks.append((qa, qb, kbuf, vbuf, rows, win, acc, False))
    for cls in range(DIL_MID):
        for r in range(DIL_MID):
            rows = pl.ds(cls * (SEQ // DIL_MID) + r, BAND, stride=DIL_MID)
            tasks.append((q40, q41, k4, v4, rows, None, acc4, True))

    def scores(task):
        qa, qb, kbuf, _, rows, win, _, _ = task
        if win is None:
            return _band_scores(qa[rows, :], qb[rows, :], kbuf[rows, :], bias_self[...])
        return _band_scores(qa[rows, :], qb[rows, :], kbuf[win, :], bias_prev[...])

    def complete(task, s):
        _, _, _, vbuf, rows, win, acc, merge = task
        res = _band_output(s, vbuf[rows if win is None else win, :])
        if merge:
            res = _merge(tuple(ref[rows, :] for ref in acc), res)
        put(acc, rows, res)

    lookahead = 4
    pending = []
    for task in tasks:
        pending.append((task, scores(task)))
        if len(pending) > lookahead:
            complete(*pending.pop(0))
    for item in pending:
        complete(*item)

    for b in range(nblk):
        rows4 = pl.ds(b * BAND, BAND)
        rows1 = pl.ds(b // DIL_MID + (b % DIL_MID) * span, BAND, stride=DIL_MID)
        o, _, l = _merge(tuple(ref[rows1, :] for ref in acc1), tuple(ref[rows4, :] for ref in acc4))
        zbuf[rows1, :] = o * (1.0 / l) * zbuf[rows1, :]

    def finish(i, carry):
        rows = pl.ds(pl.multiple_of(i * BAND, BAND), BAND)
        out_ref[rows, :] = zbuf[rows, :].astype(BF16)
        return carry

    lax.fori_loop(0, nblk, finish, 0, unroll=4)


def _dilated_attention(qa, ka, va, za, rope_a, rope_b, rope_c, batch):
    slab = pl.BlockSpec((None, None, SEQ, LANES), lambda b, j: (b, j, 0, 0))
    table = pl.BlockSpec((SEQ, LANES), lambda b, j: (0, 0))
    return pl.pallas_call(
        _dilated_kernel,
        grid=(batch, ATTN_PAIRS),
        in_specs=[slab, slab, slab, slab, table, table, table],
        out_specs=slab,
        out_shape=jax.ShapeDtypeStruct((batch, ATTN_PAIRS, SEQ, LANES), BF16),
        scratch_shapes=[pltpu.VMEM((SEQ, LANES), F32) for _ in range(15)] + [
            pltpu.VMEM((BAND, 2 * BAND), F32), pltpu.VMEM((BAND, BAND), F32)],
        compiler_params=pltpu.CompilerParams(
            dimension_semantics=("arbitrary", "arbitrary"), vmem_limit_bytes=VMEM_LIMIT_BYTES),
        name="dilated_attention",
    )(qa, ka, va, za, rope_a, rope_b, rope_c)


def _outproj_kernel(x_ref, ym_ref, ya_ref, px_ref, mem_ref, gmem_ref, wkv_ref, wout_ref,
                    gfin_ref, out_ref, kx_scr, vx_scr):
    @pl.when(pl.program_id(1) == 0)
    def _():
        mem = mem_ref[...]
        ms = jnp.mean(mem * mem, axis=-1, keepdims=True)
        mem_n = (mem * lax.rsqrt(ms + EPS) * gmem_ref[...]).astype(BF16)
        kv = _dot(mem_n, wkv_ref[...])
        kx_scr[...] = kv[:, :XATTN_WIDTH].astype(BF16)
        vx_scr[...] = kv[:, XATTN_WIDTH:].astype(BF16)

    scale = XATTN_HEAD_DIM ** -0.5 * LOG2E
    ones_b = jnp.ones((N_MEM, LANES), BF16)
    y_a = jnp.concatenate([ya_ref[j] for j in range(ATTN_PAIRS)], axis=1)
    y = _dot(y_a, wout_ref[MLSTM_WIDTH:MLSTM_WIDTH + ATTN_WIDTH, :])
    heads = []
    for h in range(XATTN_HEADS):
        sl = slice(h * XATTN_HEAD_DIM, (h + 1) * XATTN_HEAD_DIM)
        s = _dot_nt(px_ref[:, sl], kx_scr[:, sl]) * scale
        m = jnp.max(s, axis=1, keepdims=True)
        p = jnp.exp2(s - m).astype(BF16)
        o2 = _dot(p, jnp.concatenate([vx_scr[:, sl], ones_b], axis=1))
        heads.append(o2[:, :LANES] * (1.0 / o2[:, LANES:]))
        rows_m = slice(h * MLSTM_HEAD_DIM, (h + 1) * MLSTM_HEAD_DIM)
        y = y + _dot(ym_ref[:, rows_m], wout_ref[rows_m, :])
    o_x = jnp.concatenate(heads, axis=1)
    y_x = (o_x * _silu(px_ref[:, XATTN_WIDTH:].astype(F32))).astype(BF16)
    y = y + _dot(y_x, wout_ref[MLSTM_WIDTH + ATTN_WIDTH:, :])
    r = x_ref[...] + y
    ms = jnp.mean(r * r, axis=-1, keepdims=True)
    out_ref[...] = r * lax.rsqrt(ms + EPS) * gfin_ref[...]


def _output_projection(x2, y_m, y_a, px, mem, g_mem, wkv_b, wout_b, g_final, batch):
    tokens = x2.shape[0]
    tps = SEQ // ROW_TILE
    rows = lambda width: pl.BlockSpec((ROW_TILE, width), lambda b, i: (b * tps + i, 0))
    full = lambda shape: pl.BlockSpec(shape, lambda b, i: (0,) * len(shape))
    return pl.pallas_call(
        _outproj_kernel,
        grid=(batch, tps),
        in_specs=[
            rows(D_MODEL), rows(MLSTM_WIDTH),
            pl.BlockSpec((None, ATTN_PAIRS, ROW_TILE, LANES), lambda b, i: (b, 0, i, 0)),
            rows(2 * XATTN_WIDTH),
            pl.BlockSpec((None, N_MEM, D_MODEL), lambda b, i: (b, 0, 0)),
            full((1, D_MODEL)), full((D_MODEL, 2 * XATTN_WIDTH)), full((MIX_WIDTH, D_MODEL)),
            full((1, D_MODEL)),
        ],
        out_specs=rows(D_MODEL),
        out_shape=jax.ShapeDtypeStruct((tokens, D_MODEL), F32),
        scratch_shapes=[pltpu.VMEM((N_MEM, XATTN_WIDTH), BF16), pltpu.VMEM((N_MEM, XATTN_WIDTH), BF16)],
        compiler_params=pltpu.CompilerParams(
            dimension_semantics=("arbitrary", "arbitrary"), vmem_limit_bytes=VMEM_LIMIT_BYTES),
        name="out_projection",
    )(x2, y_m, y_a, px, mem, g_mem, wkv_b, wout_b, g_final)


def _block_diag_dense(w_blk):
    rows = w_blk.reshape(MLSTM_HEADS, MLSTM_HEAD_DIM, QKV_BLOCK)
    tiled = jnp.tile(rows, (1, 1, MLSTM_HEAD_DIM // QKV_BLOCK))
    idx = jnp.arange(MLSTM_HEAD_DIM) // QKV_BLOCK
    same_block = idx[:, None] == idx[None, :]
    return jnp.where(same_block[None], tiled, 0.0).astype(BF16)


def _rope_tables():
    half = ROPE_DIM // 2
    pos = jnp.arange(SEQ, dtype=F32)
    inv = ROPE_THETA ** (-jnp.arange(0, ROPE_DIM, 2, dtype=F32) / ROPE_DIM)
    ang = pos[:, None] * inv[None, :]
    cos, sin = jnp.cos(ang), jnp.sin(ang)
    ones = jnp.ones((SEQ, ATTN_HEAD_DIM - ROPE_DIM), F32)
    zeros = jnp.zeros((SEQ, ATTN_HEAD_DIM - ROPE_DIM), F32)
    zero_h = jnp.zeros((SEQ, half), F32)
    a = jnp.concatenate([cos, cos, ones], axis=1)
    b = jnp.concatenate([-sin, zero_h, zeros], axis=1)
    c = jnp.concatenate([zero_h, sin, zeros], axis=1)
    tile = lambda t: jnp.concatenate([t, t], axis=1)
    return tile(a), tile(b), tile(c)


def kernel(x, mem, g_norm, w_in, conv_w, conv_b, w_q_blk, w_k_blk, w_v_blk, w_gate, b_gate,
           g_head, skip, g_mem, w_mem_kv, w_out, g_final):
    batch = x.shape[0]
    assert x.shape[1:] == (SEQ, D_MODEL) and g_norm.shape[0] == 1
    layer = 0
    x2 = x.reshape(batch * SEQ, D_MODEL)
    wg_pad = jnp.pad(w_gate[layer], ((0, 0), (0, LANES - 2 * MLSTM_HEADS))).astype(BF16)
    bg_pad = jnp.pad(b_gate[layer], (0, LANES - 2 * MLSTM_HEADS))[None, :]
    qa, ka, va, za, px, y_m = _inproj_mlstm(
        x2, g_norm[layer][None, :], w_in[layer].astype(BF16), conv_w[layer], conv_b[layer][None, :],
        _block_diag_dense(w_q_blk[layer]), _block_diag_dense(w_k_blk[layer]),
        _block_diag_dense(w_v_blk[layer]), wg_pad, bg_pad,
        g_head[layer][None, :], skip[layer][None, :], batch)

    rope_a, rope_b, rope_c = _rope_tables()
    y_a = _dilated_attention(qa, ka, va, za, rope_a, rope_b, rope_c, batch)

    out = _output_projection(x2, y_m, y_a, px, mem,
                             g_mem[layer][None, :], w_mem_kv[layer].astype(BF16),
                             w_out[layer].astype(BF16), g_final[None, :], batch)
    return out.reshape(batch, SEQ, D_MODEL)
```

```python
import jax
import jax.numpy as jnp
from jax import lax
from jax.experimental import pallas as pl
from jax.experimental.pallas import tpu as pltpu

F32 = jnp.float32
BF16 = jnp.bfloat16

D_MODEL = 1024
SEQ = 2048
N_MEM = 256
EPS = 1e-6

MLSTM_HEADS = 4
MLSTM_HEAD_DIM = 256
MLSTM_WIDTH = 1024
QKV_BLOCK = 4
CONV_WIDTH = 4

ATTN_HEADS = 8
ATTN_HEAD_DIM = 64
ATTN_WIDTH = 512
ATTN_PAIRS = ATTN_HEADS // 2
ROPE_DIM = 16
ROPE_THETA = 500000.0
BAND = 128
DIL_MID = 4
DIL_MAX = 16

XATTN_HEADS = 4
XATTN_HEAD_DIM = 128
XATTN_WIDTH = 512

MIX_WIDTH = MLSTM_WIDTH + ATTN_WIDTH + XATTN_WIDTH
IN_WIDTH = 3 * MLSTM_WIDTH + 4 * ATTN_WIDTH + 2 * XATTN_WIDTH

LANES = 128
VMEM_LIMIT_BYTES = 56 * 1024 * 1024
NEG = -1e30
LOG2E = 1.4426950408889634

ROW_TILE = 512
PROJ_COLS = 512
MLSTM_CHUNK = 256


def _sigmoid(x):
    return 1.0 / (1.0 + jnp.exp2(x * (-LOG2E)))


def _silu(x):
    return x * _sigmoid(x)


def _log_sigmoid(x):
    return jnp.minimum(x, 0.0) - jnp.log(1.0 + jnp.exp(-jnp.abs(x)))


def _dot(a, b):
    return jnp.dot(a, b, preferred_element_type=F32)


def _dot_nt(a, b):
    return lax.dot_general(a, b, (((1,), (1,)), ((), ())), preferred_element_type=F32)


def _dot_tn(a, b):
    return lax.dot_general(a, b, (((0,), (0,)), ((), ())), preferred_element_type=F32)


def _mlstm_front(xm_ref, convw_ref, convb_ref, wq_ref, wk_ref, wv_ref, wg_ref, bg_ref, xbuf, stage):
    L = MLSTM_CHUNK
    dh = MLSTM_HEAD_DIM
    xc_ref, q_ref, k_ref, v_ref, gcol_ref, grow_ref = stage

    xbuf[8:8 + L, :] = xm_ref[...].astype(F32)
    conv = convb_ref[...] + convw_ref[3:4, :] * xbuf[8:8 + L, :]
    conv = conv + convw_ref[2:3, :] * xbuf[7:7 + L, :]
    conv = conv + convw_ref[1:2, :] * xbuf[6:6 + L, :]
    conv = conv + convw_ref[0:1, :] * xbuf[5:5 + L, :]
    xbuf[0:8, :] = xbuf[L:L + 8, :]
    xc = _silu(conv)
    xc_ref[...] = xc
    xc_b = xc.astype(BF16)
    yield

    k_scale = dh ** -0.5
    q_b, k_b, v_b = [], [], []
    for h in range(MLSTM_HEADS):
        sl = slice(h * dh, (h + 1) * dh)
        q_b.append(_dot(xc_b[:, sl], wq_ref[h]).astype(BF16))
        k = _dot(xc_b[:, sl], wk_ref[h])
        k_ref[:, sl] = k * k_scale
        k_b.append(k.astype(BF16))
        v_b.append(_dot(xm_ref[:, sl], wv_ref[h]).astype(BF16))
        q_ref[:, sl] = q_b[h]
        v_ref[:, sl] = v_b[h]
        yield

    qkv_b = jnp.concatenate(q_b + k_b + v_b, axis=1)
    gates = _dot(qkv_b, wg_ref[...]) + bg_ref[...]
    causal = lax.broadcasted_iota(jnp.int32, (L, L), 0) >= lax.broadcasted_iota(jnp.int32, (L, L), 1)
    lf = _log_sigmoid(gates) * LOG2E
    hi = lf.astype(BF16)
    rest = lf - hi.astype(F32)
    mid = rest.astype(BF16)
    lo = (rest - mid.astype(F32)).astype(BF16)
    parts = _dot(causal.astype(BF16), jnp.concatenate([hi, mid, lo], axis=1))
    b2 = parts[:, :LANES] + parts[:, LANES:2 * LANES] + parts[:, 2 * LANES:]
    w2 = gates * LOG2E - pltpu.roll(b2, LANES - MLSTM_HEADS, 1)
    lane = lax.broadcasted_iota(jnp.int32, (L, LANES), 1)
    gcol = jnp.where(lane < MLSTM_HEADS, w2, b2)
    gcol_ref[...] = gcol
    grow_ref[...] = gcol.T


def _mlstm_back(stage, zm_ref, om_ref, ghead_ref, skip_ref, y_ref, cn_scr, m_scr):
    L = MLSTM_CHUNK
    dh = MLSTM_HEAD_DIM
    xc_ref, q_ref, k_ref, v_ref, gcol_ref, grow_ref = stage
    causal = lax.broadcasted_iota(jnp.int32, (L, L), 0) >= lax.broadcasted_iota(jnp.int32, (L, L), 1)
    ones_b = jnp.ones((L, LANES), BF16)

    for h in range(MLSTM_HEADS):
        sl = slice(h * dh, (h + 1) * dh)
        w_col = gcol_ref[:, h:h + 1]
        b_col = gcol_ref[:, MLSTM_HEADS + h:MLSTM_HEADS + h + 1]
        w_row = grow_ref[h:h + 1, :]
        m_prev = m_scr[h, 0:1, 0:1]
        cn_old = cn_scr[h]
        q_b = q_ref[:, sl]
        k_f = k_ref[:, sl]
        v_aug = jnp.concatenate([v_ref[:, sl], ones_b], axis=1)

        mw = jnp.where(causal, w_row, NEG)
        mm = jnp.maximum(m_prev, jnp.max(mw, axis=1, keepdims=True))
        wts = (jnp.exp2(mw - mm) * _dot_nt(q_b, k_f.astype(BF16))).astype(BF16)
        inter = jnp.broadcast_to(jnp.exp2(m_prev - mm), (L, LANES))
        inter_c = _dot(q_b, cn_old.astype(BF16))
        intra = _dot(wts, v_aug)
        den = inter * inter_c[:, dh:] + intra[:, dh:]
        floor = jnp.broadcast_to(jnp.exp2(-(b_col + mm)), (L, LANES))
        r = 1.0 / jnp.maximum(jnp.abs(den), floor)
        h_t = (jnp.concatenate([inter, inter], axis=1) * inter_c[:, :dh] + intra[:, :dh]) * \
            jnp.concatenate([r, r], axis=1)

        mm_last = mm[L - 1:L, :]
        gk = (jnp.exp2(w_col - mm_last) * k_f).astype(BF16)
        decay = jnp.exp2(m_prev - mm_last)
        cn_scr[h] = decay * cn_old + _dot_tn(gk, v_aug)
        m_scr[h] = jnp.broadcast_to(b_col[L - 1:L, :] + mm_last, (8, LANES))
        yield

        hg = _sigmoid(om_ref[:, sl].astype(F32)) * h_t
        mu = jnp.mean(hg, axis=1, keepdims=True)
        cen = hg - mu
        var = jnp.mean(cen * cen, axis=1, keepdims=True)
        ln = cen * lax.rsqrt(var + EPS) * ghead_ref[:, sl]
        y = (ln + skip_ref[:, sl] * xc_ref[:, sl]) * _silu(zm_ref[:, sl].astype(F32))
        y_ref[:, sl] = y.astype(BF16)
        yield


def _round_robin(*gens):
    done = object()
    live = list(gens)
    while live:
        live = [g for g in live if next(g, done) is not done]


def _inproj_pieces(h_scr, w_ref, sinks):
    for col, store in sinks:
        store(_dot(h_scr[...], w_ref[:, col:col + PROJ_COLS]).astype(BF16))
        yield


def _inproj_mlstm_kernel(x_ref, g_ref, w_ref, convw_ref, convb_ref, wq_ref, wk_ref, wv_ref,
                         wg_ref, bg_ref, ghead_ref, skip_ref,
                         qa_ref, ka_ref, va_ref, za_ref, px_ref, ym_ref,
                         h_scr, pm_cur, xm_prev, zo_prev, xbuf, cn_scr, m_scr, *stages):
    L = MLSTM_CHUNK
    steps_per_seq = SEQ // (2 * L)
    j = pl.program_id(0)
    stage0, stage1 = stages[:len(stages) // 2], stages[len(stages) // 2:]
    front_args = (convw_ref, convb_ref, wq_ref, wk_ref, wv_ref, wg_ref, bg_ref, xbuf)
    back_args = (ghead_ref, skip_ref)

    @pl.when(j == 0)
    def _():
        xbuf[0:8, :] = jnp.zeros((8, MLSTM_WIDTH), F32)
        xm_prev[...] = jnp.zeros_like(xm_prev)
        zo_prev[...] = jnp.zeros_like(zo_prev)
        for ref in stage0:
            ref[...] = jnp.zeros_like(ref)

    @pl.when((j == 0) | (j % steps_per_seq == 1))
    def _():
        cn_scr[...] = jnp.zeros_like(cn_scr)
        m_scr[...] = jnp.zeros_like(m_scr)

    x = x_ref[...]
    ms = jnp.mean(x * x, axis=-1, keepdims=True)
    h_scr[...] = (x * lax.rsqrt(ms + EPS) * g_ref[...]).astype(BF16)

    def pm_sink(col):
        def store(r):
            pm_cur[:, col:col + PROJ_COLS] = r
        return store

    def pair_sink(ref, col):
        def store(r):
            for p in range(PROJ_COLS // LANES):
                ref[col // LANES + p] = r[:, p * LANES:(p + 1) * LANES]
        return store

    def px_sink(col):
        def store(r):
            px_ref[:, col:col + PROJ_COLS] = r
        return store

    base_a = 3 * MLSTM_WIDTH
    base_x = base_a + 4 * ATTN_WIDTH
    sinks_a = [(n * PROJ_COLS, pm_sink(n * PROJ_COLS)) for n in range(3 * MLSTM_WIDTH // PROJ_COLS)]
    sinks_b = [(base_a + k * ATTN_WIDTH + col, pair_sink(ref, col))
               for k, ref in enumerate((qa_ref, ka_ref, va_ref, za_ref))
               for col in range(0, ATTN_WIDTH, PROJ_COLS)]
    sinks_b += [(base_x + n * PROJ_COLS, px_sink(n * PROJ_COLS))
                for n in range(2 * XATTN_WIDTH // PROJ_COLS)]

    first, second = pl.ds(0, L), pl.ds(L, L)
    zm_cols, om_cols = pl.ds(0, MLSTM_WIDTH), pl.ds(MLSTM_WIDTH, MLSTM_WIDTH)
    _round_robin(
        _inproj_pieces(h_scr, w_ref, sinks_a),
        _mlstm_front(xm_prev, *front_args, stage1),
        _mlstm_back(stage0, zo_prev.at[first, zm_cols], zo_prev.at[first, om_cols], *back_args,
                    ym_ref.at[first], cn_scr, m_scr))

    @pl.when(j % steps_per_seq == 0)
    def _():
        xbuf[0:8, :] = jnp.zeros((8, MLSTM_WIDTH), F32)

    _round_robin(
        _inproj_pieces(h_scr, w_ref, sinks_b),
        _mlstm_front(pm_cur.at[first, pl.ds(0, MLSTM_WIDTH)], *front_args, stage0),
        _mlstm_back(stage1, zo_prev.at[second, zm_cols], zo_prev.at[second, om_cols], *back_args,
                    ym_ref.at[second], cn_scr, m_scr))

    xm_prev[...] = pm_cur[L:2 * L, 0:MLSTM_WIDTH]
    zo_prev[...] = pm_cur[:, MLSTM_WIDTH:3 * MLSTM_WIDTH]


def _inproj_mlstm(x2, g_norm, w_in_b, conv_w, conv_b, wq_d, wk_d, wv_d, wg_pad, bg_pad, g_head, skip,
                  batch):
    L = MLSTM_CHUNK
    tile = 2 * L
    tokens = x2.shape[0]
    tiles = tokens // tile
    tiles_per_seq = SEQ // tile
    assert SEQ % tile == 0
    dh = MLSTM_HEAD_DIM
    cur = lambda j: jnp.minimum(j, tiles - 1)
    full = lambda shape: pl.BlockSpec(shape, lambda j: (0,) * len(shape))
    pair_spec = pl.BlockSpec((None, ATTN_PAIRS, tile, LANES),
                             lambda j: (cur(j) // tiles_per_seq, 0, cur(j) % tiles_per_seq, 0))
    pair_shape = jax.ShapeDtypeStruct((batch, ATTN_PAIRS, SEQ, LANES), BF16)
    stage = [pltpu.VMEM((L, MLSTM_WIDTH), F32), pltpu.VMEM((L, MLSTM_WIDTH), BF16),
             pltpu.VMEM((L, MLSTM_WIDTH), F32), pltpu.VMEM((L, MLSTM_WIDTH), BF16),
             pltpu.VMEM((L, LANES), F32), pltpu.VMEM((LANES, L), F32)]
    return pl.pallas_call(
        _inproj_mlstm_kernel,
        grid=(tiles + 1,),
        in_specs=[
            pl.BlockSpec((tile, D_MODEL), lambda j: (cur(j), 0)),
            full((1, D_MODEL)),
            pl.BlockSpec((D_MODEL, IN_WIDTH), lambda j: (0, 0), pipeline_mode=pl.Buffered(1)),
            full((CONV_WIDTH, MLSTM_WIDTH)), full((1, MLSTM_WIDTH)),
            full((MLSTM_HEADS, dh, dh)), full((MLSTM_HEADS, dh, dh)), full((MLSTM_HEADS, dh, dh)),
            full((3 * MLSTM_WIDTH, LANES)), full((1, LANES)),
            full((1, MLSTM_WIDTH)), full((1, MLSTM_WIDTH)),
        ],
        out_specs=[
            pair_spec, pair_spec, pair_spec, pair_spec,
            pl.BlockSpec((tile, 2 * XATTN_WIDTH), lambda j: (cur(j), 0)),
            pl.BlockSpec((tile, MLSTM_WIDTH), lambda j: (jnp.maximum(j - 1, 0), 0)),
        ],
        out_shape=[
            pair_shape, pair_shape, pair_shape, pair_shape,
            jax.ShapeDtypeStruct((tokens, 2 * XATTN_WIDTH), BF16),
            jax.ShapeDtypeStruct((tokens, MLSTM_WIDTH), BF16),
        ],
        scratch_shapes=[
            pltpu.VMEM((tile, D_MODEL), BF16),
            pltpu.VMEM((tile, 3 * MLSTM_WIDTH), BF16),
            pltpu.VMEM((L, MLSTM_WIDTH), BF16),
            pltpu.VMEM((tile, 2 * MLSTM_WIDTH), BF16),
            pltpu.VMEM((L + 8, MLSTM_WIDTH), F32),
            pltpu.VMEM((MLSTM_HEADS, dh, dh + LANES), F32),
            pltpu.VMEM((MLSTM_HEADS, 8, LANES), F32),
        ] + stage + stage,
        compiler_params=pltpu.CompilerParams(
            dimension_semantics=("arbitrary",), vmem_limit_bytes=VMEM_LIMIT_BYTES),
        name="in_projection_mlstm",
    )(x2, g_norm, w_in_b, conv_w, conv_b, wq_d, wk_d, wv_d, wg_pad, bg_pad, g_head, skip)


def _band_scores(q_top, q_bot, kk, bias):
    q2 = jnp.concatenate([q_top, q_bot], axis=0).astype(BF16)
    return _dot_nt(q2, kk.astype(BF16)) + jnp.concatenate([bias, bias], axis=0)


def _band_output(s, vv):
    nk = vv.shape[0]
    m = jnp.max(s, axis=1, keepdims=True)
    p = jnp.exp2(s - m).astype(BF16)
    v_aug = jnp.concatenate([vv.astype(BF16), jnp.ones((nk, LANES), BF16)], axis=1)
    o2 = _dot(p, v_aug)
    first = lax.broadcasted_iota(jnp.int32, (BAND, LANES), 1) < ATTN_HEAD_DIM
    o = jnp.where(first, o2[:BAND, :LANES], o2[BAND:, :LANES])
    l_p = jnp.where(first, o2[:BAND, LANES:], o2[BAND:, LANES:])
    m_p = jnp.where(first, m[:BAND], m[BAND:])
    return o, m_p, l_p


def _merge(a, b):
    (o_a, m_a, l_a), (o_b, m_b, l_b) = a, b
    m_new = jnp.maximum(m_a, m_b)
    w_a = jnp.exp2(m_a - m_new)
    w_b = jnp.exp2(m_b - m_new)
    return o_a * w_a + o_b * w_b, m_new, l_a * w_a + l_b * w_b


def _dilated_kernel(q_ref, k_ref, v_ref, z_ref, ra_ref, rb_ref, rc_ref, out_ref,
                    q0, q1, ks, vs, q40, q41, k4, v4, a1o, a1m, a1l, a4o, a4m, a4l, zbuf,
                    bias_prev, bias_self):
    nblk = SEQ // BAND
    span = DIL_MID * BAND
    q_scale = ATTN_HEAD_DIM ** -0.5 * LOG2E

    t = lax.broadcasted_iota(jnp.int32, (BAND, 2 * BAND), 0)
    c = lax.broadcasted_iota(jnp.int32, (BAND, 2 * BAND), 1)
    bias_prev[...] = jnp.where((c >= t) & (c <= t + BAND), 0.0, NEG)
    t = lax.broadcasted_iota(jnp.int32, (BAND, BAND), 0)
    c = lax.broadcasted_iota(jnp.int32, (BAND, BAND), 1)
    bias_self[...] = jnp.where(c <= t, 0.0, NEG)

    def rope(x, rows):
        return (x * ra_ref[rows, :] + pltpu.roll(x, LANES - ROPE_DIM // 2, 1) * rb_ref[rows, :]
                + pltpu.roll(x, ROPE_DIM // 2, 1) * rc_ref[rows, :])

    def prep(i, carry):
        rows = pl.ds(pl.multiple_of(i * BAND, BAND), BAND)
        q = rope(q_ref[rows, :].astype(F32), rows) * q_scale
        first = lax.broadcasted_iota(jnp.int32, (BAND, LANES), 1) < ATTN_HEAD_DIM
        q0[rows, :] = jnp.where(first, q, 0.0)
        q1[rows, :] = jnp.where(first, 0.0, q)
        ks[rows, :] = rope(k_ref[rows, :].astype(F32), rows)
        vs[rows, :] = v_ref[rows, :].astype(F32)
        zbuf[rows, :] = _silu(z_ref[rows, :].astype(F32))
        return carry

    lax.fori_loop(0, nblk, prep, 0, unroll=4)

    def deinterleave(idx, carry):
        src = pl.ds(idx // DIL_MID + (idx % DIL_MID) * span, BAND, stride=DIL_MID)
        dst = pl.ds(pl.multiple_of(idx * BAND, BAND), BAND)
        q40[dst, :] = q0[src, :]
        q41[dst, :] = q1[src, :]
        k4[dst, :] = ks[src, :]
        v4[dst, :] = vs[src, :]
        return carry

    lax.fori_loop(0, nblk, deinterleave, 0, unroll=4)

    def put(acc, rows, res):
        for ref, val in zip(acc, res):
            ref[rows, :] = val

    acc1 = (a1o, a1m, a1l)
    acc4 = (a4o, a4m, a4l)

    tasks = []
    for qa, qb, kbuf, vbuf, acc, per_class in ((q0, q1, ks, vs, acc1, nblk),
                                                (q40, q41, k4, v4, acc4, nblk // DIL_MID)):
        for b in range(nblk):
            rows = pl.ds(b * BAND, BAND)
            win = None if b % per_class == 0 else pl.ds((b - 1) * BAND, 2 * BAND)
            tasks.append((qa, qb, kbuf, vbuf, rows, win, acc, False))
    for cls in range(DIL_MID):
        for r in range(DIL_MID):
            rows = pl.ds(cls * (SEQ // DIL_MID) + r, BAND, stride=DIL_MID)
            tasks.append((q40, q41, k4, v4, rows, None, acc4, True))

    def scores(task):
        qa, qb, kbuf, _, rows, win, _, _ = task
        if win is None:
            return _band_scores(qa[rows, :], qb[rows, :], kbuf[rows, :], bias_self[...])
        return _band_scores(qa[rows, :], qb[rows, :], kbuf[win, :], bias_prev[...])

    def complete(task, s):
        _, _, _, vbuf, rows, win, acc, merge = task
        res = _band_output(s, vbuf[rows if win is None else win, :])
        if merge:
            res = _merge(tuple(ref[rows, :] for ref in acc), res)
        put(acc, rows, res)

    lookahead = 2
    pending = []
    for task in tasks:
        pending.append((task, scores(task)))
        if len(pending) > lookahead:
            complete(*pending.pop(0))
    for item in pending:
        complete(*item)

    for b in range(nblk):
        rows4 = pl.ds(b * BAND, BAND)
        rows1 = pl.ds(b // DIL_MID + (b % DIL_MID) * span, BAND, stride=DIL_MID)
        o, _, l = _merge(tuple(ref[rows1, :] for ref in acc1), tuple(ref[rows4, :] for ref in acc4))
        zbuf[rows1, :] = o * (1.0 / l) * zbuf[rows1, :]

    def finish(i, carry):
        rows = pl.ds(pl.multiple_of(i * BAND, BAND), BAND)
        out_ref[rows, :] = zbuf[rows, :].astype(BF16)
        return carry

    lax.fori_loop(0, nblk, finish, 0, unroll=4)


def _dilated_attention(qa, ka, va, za, rope_a, rope_b, rope_c, batch):
    slab = pl.BlockSpec((None, None, SEQ, LANES), lambda b, j: (b, j, 0, 0))
    table = pl.BlockSpec((SEQ, LANES), lambda b, j: (0, 0))
    return pl.pallas_call(
        _dilated_kernel,
        grid=(batch, ATTN_PAIRS),
        in_specs=[slab, slab, slab, slab, table, table, table],
        out_specs=slab,
        out_shape=jax.ShapeDtypeStruct((batch, ATTN_PAIRS, SEQ, LANES), BF16),
        scratch_shapes=[pltpu.VMEM((SEQ, LANES), F32) for _ in range(15)] + [
            pltpu.VMEM((BAND, 2 * BAND), F32), pltpu.VMEM((BAND, BAND), F32)],
        compiler_params=pltpu.CompilerParams(
            dimension_semantics=("arbitrary", "arbitrary"), vmem_limit_bytes=VMEM_LIMIT_BYTES),
        name="dilated_attention",
    )(qa, ka, va, za, rope_a, rope_b, rope_c)


def _outproj_kernel(x_ref, ym_ref, ya_ref, px_ref, mem_ref, gmem_ref, wkv_ref, wout_ref,
                    gfin_ref, out_ref, kx_scr, vx_scr):
    @pl.when(pl.program_id(1) == 0)
    def _():
        mem = mem_ref[...]
        ms = jnp.mean(mem * mem, axis=-1, keepdims=True)
        mem_n = (mem * lax.rsqrt(ms + EPS) * gmem_ref[...]).astype(BF16)
        kv = _dot(mem_n, wkv_ref[...])
        kx_scr[...] = kv[:, :XATTN_WIDTH].astype(BF16)
        vx_scr[...] = kv[:, XATTN_WIDTH:].astype(BF16)

    scale = XATTN_HEAD_DIM ** -0.5 * LOG2E
    ones_b = jnp.ones((N_MEM, LANES), BF16)
    y_a = jnp.concatenate([ya_ref[j] for j in range(ATTN_PAIRS)], axis=1)
    y = _dot(y_a, wout_ref[MLSTM_WIDTH:MLSTM_WIDTH + ATTN_WIDTH, :])
    heads = []
    for h in range(XATTN_HEADS):
        sl = slice(h * XATTN_HEAD_DIM, (h + 1) * XATTN_HEAD_DIM)
        s = _dot_nt(px_ref[:, sl], kx_scr[:, sl]) * scale
        m = jnp.max(s, axis=1, keepdims=True)
        p = jnp.exp2(s - m).astype(BF16)
        o2 = _dot(p, jnp.concatenate([vx_scr[:, sl], ones_b], axis=1))
        heads.append(o2[:, :LANES] * (1.0 / o2[:, LANES:]))
        rows_m = slice(h * MLSTM_HEAD_DIM, (h + 1) * MLSTM_HEAD_DIM)
        y = y + _dot(ym_ref[:, rows_m], wout_ref[rows_m, :])
    o_x = jnp.concatenate(heads, axis=1)
    y_x = (o_x * _silu(px_ref[:, XATTN_WIDTH:].astype(F32))).astype(BF16)
    y = y + _dot(y_x, wout_ref[MLSTM_WIDTH + ATTN_WIDTH:, :])
    r = x_ref[...] + y
    ms = jnp.mean(r * r, axis=-1, keepdims=True)
    out_ref[...] = r * lax.rsqrt(ms + EPS) * gfin_ref[...]


def _output_projection(x2, y_m, y_a, px, mem, g_mem, wkv_b, wout_b, g_final, batch):
    tokens = x2.shape[0]
    tps = SEQ // ROW_TILE
    rows = lambda width: pl.BlockSpec((ROW_TILE, width), lambda b, i: (b * tps + i, 0))
    full = lambda shape: pl.BlockSpec(shape, lambda b, i: (0,) * len(shape))
    return pl.pallas_call(
        _outproj_kernel,
        grid=(batch, tps),
        in_specs=[
            rows(D_MODEL), rows(MLSTM_WIDTH),
            pl.BlockSpec((None, ATTN_PAIRS, ROW_TILE, LANES), lambda b, i: (b, 0, i, 0)),
            rows(2 * XATTN_WIDTH),
            pl.BlockSpec((None, N_MEM, D_MODEL), lambda b, i: (b, 0, 0)),
            full((1, D_MODEL)), full((D_MODEL, 2 * XATTN_WIDTH)), full((MIX_WIDTH, D_MODEL)),
            full((1, D_MODEL)),
        ],
        out_specs=rows(D_MODEL),
        out_shape=jax.ShapeDtypeStruct((tokens, D_MODEL), F32),
        scratch_shapes=[pltpu.VMEM((N_MEM, XATTN_WIDTH), BF16), pltpu.VMEM((N_MEM, XATTN_WIDTH), BF16)],
        compiler_params=pltpu.CompilerParams(
            dimension_semantics=("arbitrary", "arbitrary"), vmem_limit_bytes=VMEM_LIMIT_BYTES),
        name="out_projection",
    )(x2, y_m, y_a, px, mem, g_mem, wkv_b, wout_b, g_final)


def _block_diag_dense(w_blk):
    rows = w_blk.reshape(MLSTM_HEADS, MLSTM_HEAD_DIM, QKV_BLOCK)
    tiled = jnp.tile(rows, (1, 1, MLSTM_HEAD_DIM // QKV_BLOCK))
    idx = jnp.arange(MLSTM_HEAD_DIM) // QKV_BLOCK
    same_block = idx[:, None] == idx[None, :]
    return jnp.where(same_block[None], tiled, 0.0).astype(BF16)


def _rope_tables():
    half = ROPE_DIM // 2
    pos = jnp.arange(SEQ, dtype=F32)
    inv = ROPE_THETA ** (-jnp.arange(0, ROPE_DIM, 2, dtype=F32) / ROPE_DIM)
    ang = pos[:, None] * inv[None, :]
    cos, sin = jnp.cos(ang), jnp.sin(ang)
    ones = jnp.ones((SEQ, ATTN_HEAD_DIM - ROPE_DIM), F32)
    zeros = jnp.zeros((SEQ, ATTN_HEAD_DIM - ROPE_DIM), F32)
    zero_h = jnp.zeros((SEQ, half), F32)
    a = jnp.concatenate([cos, cos, ones], axis=1)
    b = jnp.concatenate([-sin, zero_h, zeros], axis=1)
    c = jnp.concatenate([zero_h, sin, zeros], axis=1)
    tile = lambda t: jnp.concatenate([t, t], axis=1)
    return tile(a), tile(b), tile(c)


def kernel(x, mem, g_norm, w_in, conv_w, conv_b, w_q_blk, w_k_blk, w_v_blk, w_gate, b_gate,
           g_head, skip, g_mem, w_mem_kv, w_out, g_final):
    batch = x.shape[0]
    assert x.shape[1:] == (SEQ, D_MODEL) and g_norm.shape[0] == 1
    layer = 0
    x2 = x.reshape(batch * SEQ, D_MODEL)
    wg_pad = jnp.pad(w_gate[layer], ((0, 0), (0, LANES - 2 * MLSTM_HEADS))).astype(BF16)
    bg_pad = jnp.pad(b_gate[layer], (0, LANES - 2 * MLSTM_HEADS))[None, :]
    qa, ka, va, za, px, y_m = _inproj_mlstm(
        x2, g_norm[layer][None, :], w_in[layer].astype(BF16), conv_w[layer], conv_b[layer][None, :],
        _block_diag_dense(w_q_blk[layer]), _block_diag_dense(w_k_blk[layer]),
        _block_diag_dense(w_v_blk[layer]), wg_pad, bg_pad,
        g_head[layer][None, :], skip[layer][None, :], batch)

    rope_a, rope_b, rope_c = _rope_tables()
    y_a = _dilated_attention(qa, ka, va, za, rope_a, rope_b, rope_c, batch)

    out = _output_projection(x2, y_m, y_a, px, mem,
                             g_mem[layer][None, :], w_mem_kv[layer].astype(BF16),
                             w_out[layer].astype(BF16), g_final[None, :], batch)
    return out.reshape(batch, SEQ, D_MODEL)
```

```python
import jax
import jax.numpy as jnp
from jax import lax
from jax.experimental import pallas as pl
from jax.experimental.pallas import tpu as pltpu

F32 = jnp.float32
BF16 = jnp.bfloat16

D_MODEL = 1024
SEQ = 2048
N_MEM = 256
EPS = 1e-6

MLSTM_HEADS = 4
MLSTM_HEAD_DIM = 256
MLSTM_WIDTH = 1024
QKV_BLOCK = 4
CONV_WIDTH = 4

ATTN_HEADS = 8
ATTN_HEAD_DIM = 64
ATTN_WIDTH = 512
ATTN_PAIRS = ATTN_HEADS // 2
ROPE_DIM = 16
ROPE_THETA = 500000.0
BAND = 128
DIL_MID = 4
DIL_MAX = 16

XATTN_HEADS = 4
XATTN_HEAD_DIM = 128
XATTN_WIDTH = 512

MIX_WIDTH = MLSTM_WIDTH + ATTN_WIDTH + XATTN_WIDTH
IN_WIDTH = 3 * MLSTM_WIDTH + 4 * ATTN_WIDTH + 2 * XATTN_WIDTH

LANES = 128
VMEM_LIMIT_BYTES = 56 * 1024 * 1024
NEG = -1e30
LOG2E = 1.4426950408889634

ROW_TILE = 1024
PROJ_COLS = 512
MLSTM_CHUNK = 256


def _sigmoid(x):
    return 1.0 / (1.0 + jnp.exp2(x * (-LOG2E)))


def _silu(x):
    return x * _sigmoid(x)


def _log_sigmoid(x):
    return jnp.minimum(x, 0.0) - jnp.log(1.0 + jnp.exp(-jnp.abs(x)))


def _dot(a, b):
    return jnp.dot(a, b, preferred_element_type=F32)


def _dot_nt(a, b):
    return lax.dot_general(a, b, (((1,), (1,)), ((), ())), preferred_element_type=F32)


def _dot_tn(a, b):
    return lax.dot_general(a, b, (((0,), (0,)), ((), ())), preferred_element_type=F32)


def _mlstm_front(xm_ref, convw_ref, convb_ref, wq_ref, wk_ref, wv_ref, wg_ref, bg_ref, xbuf, stage):
    L = MLSTM_CHUNK
    dh = MLSTM_HEAD_DIM
    xc_ref, q_ref, k_ref, v_ref, gcol_ref, grow_ref = stage

    xbuf[8:8 + L, :] = xm_ref[...].astype(F32)
    conv = convb_ref[...] + convw_ref[3:4, :] * xbuf[8:8 + L, :]
    conv = conv + convw_ref[2:3, :] * xbuf[7:7 + L, :]
    conv = conv + convw_ref[1:2, :] * xbuf[6:6 + L, :]
    conv = conv + convw_ref[0:1, :] * xbuf[5:5 + L, :]
    xbuf[0:8, :] = xbuf[L:L + 8, :]
    xc = _silu(conv)
    xc_ref[...] = xc
    xc_b = xc.astype(BF16)
    yield

    k_scale = dh ** -0.5
    q_b, k_b, v_b = [], [], []
    for h in range(MLSTM_HEADS):
        sl = slice(h * dh, (h + 1) * dh)
        q_b.append(_dot(xc_b[:, sl], wq_ref[h]).astype(BF16))
        k = _dot(xc_b[:, sl], wk_ref[h])
        k_ref[:, sl] = k * k_scale
        k_b.append(k.astype(BF16))
        v_b.append(_dot(xm_ref[:, sl], wv_ref[h]).astype(BF16))
        q_ref[:, sl] = q_b[h]
        v_ref[:, sl] = v_b[h]
        yield

    qkv_b = jnp.concatenate(q_b + k_b + v_b, axis=1)
    gates = _dot(qkv_b, wg_ref[...]) + bg_ref[...]
    causal = lax.broadcasted_iota(jnp.int32, (L, L), 0) >= lax.broadcasted_iota(jnp.int32, (L, L), 1)
    b2 = jnp.dot(causal.astype(F32), _log_sigmoid(gates) * LOG2E, precision=lax.Precision.HIGHEST,
                 preferred_element_type=F32)
    w2 = gates * LOG2E - pltpu.roll(b2, LANES - MLSTM_HEADS, 1)
    lane = lax.broadcasted_iota(jnp.int32, (L, LANES), 1)
    gcol = jnp.where(lane < MLSTM_HEADS, w2, b2)
    gcol_ref[...] = gcol
    grow_ref[...] = gcol.T


def _mlstm_back(stage, zm_ref, om_ref, ghead_ref, skip_ref, y_ref, cn_scr, m_scr):
    L = MLSTM_CHUNK
    dh = MLSTM_HEAD_DIM
    xc_ref, q_ref, k_ref, v_ref, gcol_ref, grow_ref = stage
    causal = lax.broadcasted_iota(jnp.int32, (L, L), 0) >= lax.broadcasted_iota(jnp.int32, (L, L), 1)
    ones_b = jnp.ones((L, LANES), BF16)

    for h in range(MLSTM_HEADS):
        sl = slice(h * dh, (h + 1) * dh)
        w_col = gcol_ref[:, h:h + 1]
        b_col = gcol_ref[:, MLSTM_HEADS + h:MLSTM_HEADS + h + 1]
        w_row = grow_ref[h:h + 1, :]
        m_prev = m_scr[h, 0:1, 0:1]
        cn_old = cn_scr[h]
        q_b = q_ref[:, sl]
        k_f = k_ref[:, sl]
        v_aug = jnp.concatenate([v_ref[:, sl], ones_b], axis=1)

        mw = jnp.where(causal, w_row, NEG)
        mm = jnp.maximum(m_prev, jnp.max(mw, axis=1, keepdims=True))
        wts = (jnp.exp2(mw - mm) * _dot_nt(q_b, k_f.astype(BF16))).astype(BF16)
        inter = jnp.broadcast_to(jnp.exp2(m_prev - mm), (L, LANES))
        inter_c = _dot(q_b, cn_old.astype(BF16))
        intra = _dot(wts, v_aug)
        den = inter * inter_c[:, dh:] + intra[:, dh:]
        floor = jnp.broadcast_to(jnp.exp2(-(b_col + mm)), (L, LANES))
        r = 1.0 / jnp.maximum(jnp.abs(den), floor)
        h_t = (jnp.concatenate([inter, inter], axis=1) * inter_c[:, :dh] + intra[:, :dh]) * \
            jnp.concatenate([r, r], axis=1)

        mm_last = mm[L - 1:L, :]
        gk = (jnp.exp2(w_col - mm_last) * k_f).astype(BF16)
        decay = jnp.exp2(m_prev - mm_last)
        cn_scr[h] = decay * cn_old + _dot_tn(gk, v_aug)
        m_scr[h] = jnp.broadcast_to(b_col[L - 1:L, :] + mm_last, (8, LANES))
        yield

        hg = _sigmoid(om_ref[:, sl].astype(F32)) * h_t
        mu = jnp.mean(hg, axis=1, keepdims=True)
        cen = hg - mu
        var = jnp.mean(cen * cen, axis=1, keepdims=True)
        ln = cen * lax.rsqrt(var + EPS) * ghead_ref[:, sl]
        y = (ln + skip_ref[:, sl] * xc_ref[:, sl]) * _silu(zm_ref[:, sl].astype(F32))
        y_ref[:, sl] = y.astype(BF16)
        yield


def _round_robin(*gens):
    done = object()
    live = list(gens)
    while live:
        live = [g for g in live if next(g, done) is not done]


def _inproj_pieces(h_scr, w_ref, sinks):
    for col, store in sinks:
        store(_dot(h_scr[...], w_ref[:, col:col + PROJ_COLS]).astype(BF16))
        yield


def _inproj_mlstm_kernel(x_ref, g_ref, w_ref, convw_ref, convb_ref, wq_ref, wk_ref, wv_ref,
                         wg_ref, bg_ref, ghead_ref, skip_ref,
                         qa_ref, ka_ref, va_ref, za_ref, px_ref, ym_ref,
                         h_scr, pm_cur, xm_prev, zo_prev, xbuf, cn_scr, m_scr, *stages):
    L = MLSTM_CHUNK
    steps_per_seq = SEQ // (2 * L)
    j = pl.program_id(0)
    stage0, stage1 = stages[:len(stages) // 2], stages[len(stages) // 2:]
    front_args = (convw_ref, convb_ref, wq_ref, wk_ref, wv_ref, wg_ref, bg_ref, xbuf)
    back_args = (ghead_ref, skip_ref)

    @pl.when(j == 0)
    def _():
        xbuf[0:8, :] = jnp.zeros((8, MLSTM_WIDTH), F32)
        xm_prev[...] = jnp.zeros_like(xm_prev)
        zo_prev[...] = jnp.zeros_like(zo_prev)
        for ref in stage0:
            ref[...] = jnp.zeros_like(ref)

    @pl.when((j == 0) | (j % steps_per_seq == 1))
    def _():
        cn_scr[...] = jnp.zeros_like(cn_scr)
        m_scr[...] = jnp.zeros_like(m_scr)

    x = x_ref[...]
    ms = jnp.mean(x * x, axis=-1, keepdims=True)
    h_scr[...] = (x * lax.rsqrt(ms + EPS) * g_ref[...]).astype(BF16)

    def pm_sink(col):
        def store(r):
            pm_cur[:, col:col + PROJ_COLS] = r
        return store

    def pair_sink(ref, col):
        def store(r):
            for p in range(PROJ_COLS // LANES):
                ref[col // LANES + p] = r[:, p * LANES:(p + 1) * LANES]
        return store

    def px_sink(col):
        def store(r):
            px_ref[:, col:col + PROJ_COLS] = r
        return store

    base_a = 3 * MLSTM_WIDTH
    base_x = base_a + 4 * ATTN_WIDTH
    sinks_a = [(n * PROJ_COLS, pm_sink(n * PROJ_COLS)) for n in range(3 * MLSTM_WIDTH // PROJ_COLS)]
    sinks_b = [(base_a + k * ATTN_WIDTH + col, pair_sink(ref, col))
               for k, ref in enumerate((qa_ref, ka_ref, va_ref, za_ref))
               for col in range(0, ATTN_WIDTH, PROJ_COLS)]
    sinks_b += [(base_x + n * PROJ_COLS, px_sink(n * PROJ_COLS))
                for n in range(2 * XATTN_WIDTH // PROJ_COLS)]

    first, second = pl.ds(0, L), pl.ds(L, L)
    zm_cols, om_cols = pl.ds(0, MLSTM_WIDTH), pl.ds(MLSTM_WIDTH, MLSTM_WIDTH)
    _round_robin(
        _inproj_pieces(h_scr, w_ref, sinks_a),
        _mlstm_front(xm_prev, *front_args, stage1),
        _mlstm_back(stage0, zo_prev.at[first, zm_cols], zo_prev.at[first, om_cols], *back_args,
                    ym_ref.at[first], cn_scr, m_scr))

    @pl.when(j % steps_per_seq == 0)
    def _():
        xbuf[0:8, :] = jnp.zeros((8, MLSTM_WIDTH), F32)

    _round_robin(
        _inproj_pieces(h_scr, w_ref, sinks_b),
        _mlstm_front(pm_cur.at[first, pl.ds(0, MLSTM_WIDTH)], *front_args, stage0),
        _mlstm_back(stage1, zo_prev.at[second, zm_cols], zo_prev.at[second, om_cols], *back_args,
                    ym_ref.at[second], cn_scr, m_scr))

    xm_prev[...] = pm_cur[L:2 * L, 0:MLSTM_WIDTH]
    zo_prev[...] = pm_cur[:, MLSTM_WIDTH:3 * MLSTM_WIDTH]


def _inproj_mlstm(x2, g_norm, w_in_b, conv_w, conv_b, wq_d, wk_d, wv_d, wg_pad, bg_pad, g_head, skip,
                  batch):
    L = MLSTM_CHUNK
    tile = 2 * L
    tokens = x2.shape[0]
    tiles = tokens // tile
    tiles_per_seq = SEQ // tile
    assert SEQ % tile == 0
    dh = MLSTM_HEAD_DIM
    cur = lambda j: jnp.minimum(j, tiles - 1)
    full = lambda shape: pl.BlockSpec(shape, lambda j: (0,) * len(shape))
    pair_spec = pl.BlockSpec((None, ATTN_PAIRS, tile, LANES),
                             lambda j: (cur(j) // tiles_per_seq, 0, cur(j) % tiles_per_seq, 0))
    pair_shape = jax.ShapeDtypeStruct((batch, ATTN_PAIRS, SEQ, LANES), BF16)
    stage = [pltpu.VMEM((L, MLSTM_WIDTH), F32), pltpu.VMEM((L, MLSTM_WIDTH), BF16),
             pltpu.VMEM((L, MLSTM_WIDTH), F32), pltpu.VMEM((L, MLSTM_WIDTH), BF16),
             pltpu.VMEM((L, LANES), F32), pltpu.VMEM((LANES, L), F32)]
    return pl.pallas_call(
        _inproj_mlstm_kernel,
        grid=(tiles + 1,),
        in_specs=[
            pl.BlockSpec((tile, D_MODEL), lambda j: (cur(j), 0)),
            full((1, D_MODEL)),
            pl.BlockSpec((D_MODEL, IN_WIDTH), lambda j: (0, 0), pipeline_mode=pl.Buffered(1)),
            full((CONV_WIDTH, MLSTM_WIDTH)), full((1, MLSTM_WIDTH)),
            full((MLSTM_HEADS, dh, dh)), full((MLSTM_HEADS, dh, dh)), full((MLSTM_HEADS, dh, dh)),
            full((3 * MLSTM_WIDTH, LANES)), full((1, LANES)),
            full((1, MLSTM_WIDTH)), full((1, MLSTM_WIDTH)),
        ],
        out_specs=[
            pair_spec, pair_spec, pair_spec, pair_spec,
            pl.BlockSpec((tile, 2 * XATTN_WIDTH), lambda j: (cur(j), 0)),
            pl.BlockSpec((tile, MLSTM_WIDTH), lambda j: (jnp.maximum(j - 1, 0), 0)),
        ],
        out_shape=[
            pair_shape, pair_shape, pair_shape, pair_shape,
            jax.ShapeDtypeStruct((tokens, 2 * XATTN_WIDTH), BF16),
            jax.ShapeDtypeStruct((tokens, MLSTM_WIDTH), BF16),
        ],
        scratch_shapes=[
            pltpu.VMEM((tile, D_MODEL), BF16),
            pltpu.VMEM((tile, 3 * MLSTM_WIDTH), BF16),
            pltpu.VMEM((L, MLSTM_WIDTH), BF16),
            pltpu.VMEM((tile, 2 * MLSTM_WIDTH), BF16),
            pltpu.VMEM((L + 8, MLSTM_WIDTH), F32),
            pltpu.VMEM((MLSTM_HEADS, dh, dh + LANES), F32),
            pltpu.VMEM((MLSTM_HEADS, 8, LANES), F32),
        ] + stage + stage,
        compiler_params=pltpu.CompilerParams(
            dimension_semantics=("arbitrary",), vmem_limit_bytes=VMEM_LIMIT_BYTES),
        name="in_projection_mlstm",
    )(x2, g_norm, w_in_b, conv_w, conv_b, wq_d, wk_d, wv_d, wg_pad, bg_pad, g_head, skip)


def _band_scores(q_top, q_bot, kk, bias):
    q2 = jnp.concatenate([q_top, q_bot], axis=0).astype(BF16)
    return _dot_nt(q2, kk.astype(BF16)) + jnp.concatenate([bias, bias], axis=0)


def _band_output(s, vv):
    nk = vv.shape[0]
    m = jnp.max(s, axis=1, keepdims=True)
    p = jnp.exp2(s - m).astype(BF16)
    v_aug = jnp.concatenate([vv.astype(BF16), jnp.ones((nk, LANES), BF16)], axis=1)
    o2 = _dot(p, v_aug)
    first = lax.broadcasted_iota(jnp.int32, (BAND, LANES), 1) < ATTN_HEAD_DIM
    o = jnp.where(first, o2[:BAND, :LANES], o2[BAND:, :LANES])
    l_p = jnp.where(first, o2[:BAND, LANES:], o2[BAND:, LANES:])
    m_p = jnp.where(first, m[:BAND], m[BAND:])
    return o, m_p, l_p


def _merge(a, b):
    (o_a, m_a, l_a), (o_b, m_b, l_b) = a, b
    m_new = jnp.maximum(m_a, m_b)
    w_a = jnp.exp2(m_a - m_new)
    w_b = jnp.exp2(m_b - m_new)
    return o_a * w_a + o_b * w_b, m_new, l_a * w_a + l_b * w_b


def _dilated_kernel(q_ref, k_ref, v_ref, z_ref, ra_ref, rb_ref, rc_ref, out_ref,
                    q0, q1, ks, vs, q40, q41, k4, v4, a1o, a1m, a1l, a4o, a4m, a4l, zbuf,
                    bias_prev, bias_self):
    nblk = SEQ // BAND
    span = DIL_MID * BAND
    q_scale = ATTN_HEAD_DIM ** -0.5 * LOG2E

    t = lax.broadcasted_iota(jnp.int32, (BAND, 2 * BAND), 0)
    c = lax.broadcasted_iota(jnp.int32, (BAND, 2 * BAND), 1)
    bias_prev[...] = jnp.where((c >= t) & (c <= t + BAND), 0.0, NEG)
    t = lax.broadcasted_iota(jnp.int32, (BAND, BAND), 0)
    c = lax.broadcasted_iota(jnp.int32, (BAND, BAND), 1)
    bias_self[...] = jnp.where(c <= t, 0.0, NEG)

    def rope(x, rows):
        return (x * ra_ref[rows, :] + pltpu.roll(x, LANES - ROPE_DIM // 2, 1) * rb_ref[rows, :]
                + pltpu.roll(x, ROPE_DIM // 2, 1) * rc_ref[rows, :])

    def prep(i, carry):
        rows = pl.ds(pl.multiple_of(i * BAND, BAND), BAND)
        q = rope(q_ref[rows, :].astype(F32), rows) * q_scale
        first = lax.broadcasted_iota(jnp.int32, (BAND, LANES), 1) < ATTN_HEAD_DIM
        q0[rows, :] = jnp.where(first, q, 0.0)
        q1[rows, :] = jnp.where(first, 0.0, q)
        ks[rows, :] = rope(k_ref[rows, :].astype(F32), rows)
        vs[rows, :] = v_ref[rows, :].astype(F32)
        zbuf[rows, :] = _silu(z_ref[rows, :].astype(F32))
        return carry

    lax.fori_loop(0, nblk, prep, 0, unroll=4)

    def deinterleave(idx, carry):
        src = pl.ds(idx // DIL_MID + (idx % DIL_MID) * span, BAND, stride=DIL_MID)
        dst = pl.ds(pl.multiple_of(idx * BAND, BAND), BAND)
        q40[dst, :] = q0[src, :]
        q41[dst, :] = q1[src, :]
        k4[dst, :] = ks[src, :]
        v4[dst, :] = vs[src, :]
        return carry

    lax.fori_loop(0, nblk, deinterleave, 0, unroll=True)

    def put(acc, rows, res):
        for ref, val in zip(acc, res):
            ref[rows, :] = val

    acc1 = (a1o, a1m, a1l)
    acc4 = (a4o, a4m, a4l)

    tasks = []
    for qa, qb, kbuf, vbuf, acc, per_class in ((q0, q1, ks, vs, acc1, nblk),
                                                (q40, q41, k4, v4, acc4, nblk // DIL_MID)):
        for b in range(nblk):
            rows = pl.ds(b * BAND, BAND)
            win = None if b % per_class == 0 else pl.ds((b - 1) * BAND, 2 * BAND)
            tasks.append((qa, qb, kbuf, vbuf, rows, win, acc, False))
    for cls in range(DIL_MID):
        for r in range(DIL_MID):
            rows = pl.ds(cls * (SEQ // DIL_MID) + r, BAND, stride=DIL_MID)
            tasks.append((q40, q41, k4, v4, rows, None, acc4, True))

    def scores(task):
        qa, qb, kbuf, _, rows, win, _, _ = task
        if win is None:
            return _band_scores(qa[rows, :], qb[rows, :], kbuf[rows, :], bias_self[...])
        return _band_scores(qa[rows, :], qb[rows, :], kbuf[win, :], bias_prev[...])

    def complete(task, s):
        _, _, _, vbuf, rows, win, acc, merge = task
        res = _band_output(s, vbuf[rows if win is None else win, :])
        if merge:
            res = _merge(tuple(ref[rows, :] for ref in acc), res)
        put(acc, rows, res)

    lookahead = 4
    pending = []
    for task in tasks:
        pending.append((task, scores(task)))
        if len(pending) > lookahead:
            complete(*pending.pop(0))
    for item in pending:
        complete(*item)

    for b in range(nblk):
        rows4 = pl.ds(b * BAND, BAND)
        rows1 = pl.ds(b // DIL_MID + (b % DIL_MID) * span, BAND, stride=DIL_MID)
        o, _, l = _merge(tuple(ref[rows1, :] for ref in acc1), tuple(ref[rows4, :] for ref in acc4))
        zbuf[rows1, :] = o * (1.0 / l) * zbuf[rows1, :]

    def finish(i, carry):
        rows = pl.ds(pl.multiple_of(i * BAND, BAND), BAND)
        out_ref[rows, :] = zbuf[rows, :].astype(BF16)
        return carry

    lax.fori_loop(0, nblk, finish, 0, unroll=4)


def _dilated_attention(qa, ka, va, za, rope_a, rope_b, rope_c, batch):
    slab = pl.BlockSpec((None, None, SEQ, LANES), lambda b, j: (b, j, 0, 0))
    table = pl.BlockSpec((SEQ, LANES), lambda b, j: (0, 0))
    return pl.pallas_call(
        _dilated_kernel,
        grid=(batch, ATTN_PAIRS),
        in_specs=[slab, slab, slab, slab, table, table, table],
        out_specs=slab,
        out_shape=jax.ShapeDtypeStruct((batch, ATTN_PAIRS, SEQ, LANES), BF16),
        scratch_shapes=[pltpu.VMEM((SEQ, LANES), F32) for _ in range(15)] + [
            pltpu.VMEM((BAND, 2 * BAND), F32), pltpu.VMEM((BAND, BAND), F32)],
        compiler_params=pltpu.CompilerParams(
            dimension_semantics=("arbitrary", "arbitrary"), vmem_limit_bytes=VMEM_LIMIT_BYTES),
        name="dilated_attention",
    )(qa, ka, va, za, rope_a, rope_b, rope_c)


def _outproj_kernel(x_ref, ym_ref, ya_ref, px_ref, mem_ref, gmem_ref, wkv_ref, wout_ref,
                    gfin_ref, out_ref, kx_scr, vx_scr):
    @pl.when(pl.program_id(1) == 0)
    def _():
        mem = mem_ref[...]
        ms = jnp.mean(mem * mem, axis=-1, keepdims=True)
        mem_n = (mem * lax.rsqrt(ms + EPS) * gmem_ref[...]).astype(BF16)
        kv = _dot(mem_n, wkv_ref[...])
        kx_scr[...] = kv[:, :XATTN_WIDTH].astype(BF16)
        vx_scr[...] = kv[:, XATTN_WIDTH:].astype(BF16)

    scale = XATTN_HEAD_DIM ** -0.5 * LOG2E
    ones_b = jnp.ones((N_MEM, LANES), BF16)
    y_a = jnp.concatenate([ya_ref[j] for j in range(ATTN_PAIRS)], axis=1)
    y = _dot(y_a, wout_ref[MLSTM_WIDTH:MLSTM_WIDTH + ATTN_WIDTH, :])
    heads = []
    for h in range(XATTN_HEADS):
        sl = slice(h * XATTN_HEAD_DIM, (h + 1) * XATTN_HEAD_DIM)
        s = _dot_nt(px_ref[:, sl], kx_scr[:, sl]) * scale
        m = jnp.max(s, axis=1, keepdims=True)
        p = jnp.exp2(s - m).astype(BF16)
        o2 = _dot(p, jnp.concatenate([vx_scr[:, sl], ones_b], axis=1))
        heads.append(o2[:, :LANES] * (1.0 / o2[:, LANES:]))
        rows_m = slice(h * MLSTM_HEAD_DIM, (h + 1) * MLSTM_HEAD_DIM)
        y = y + _dot(ym_ref[:, rows_m], wout_ref[rows_m, :])
    o_x = jnp.concatenate(heads, axis=1)
    y_x = (o_x * _silu(px_ref[:, XATTN_WIDTH:].astype(F32))).astype(BF16)
    y = y + _dot(y_x, wout_ref[MLSTM_WIDTH + ATTN_WIDTH:, :])
    r = x_ref[...] + y
    ms = jnp.mean(r * r, axis=-1, keepdims=True)
    out_ref[...] = r * lax.rsqrt(ms + EPS) * gfin_ref[...]


def _output_projection(x2, y_m, y_a, px, mem, g_mem, wkv_b, wout_b, g_final, batch):
    tokens = x2.shape[0]
    tps = SEQ // ROW_TILE
    rows = lambda width: pl.BlockSpec((ROW_TILE, width), lambda b, i: (b * tps + i, 0))
    full = lambda shape: pl.BlockSpec(shape, lambda b, i: (0,) * len(shape))
    return pl.pallas_call(
        _outproj_kernel,
        grid=(batch, tps),
        in_specs=[
            rows(D_MODEL), rows(MLSTM_WIDTH),
            pl.BlockSpec((None, ATTN_PAIRS, ROW_TILE, LANES), lambda b, i: (b, 0, i, 0)),
            rows(2 * XATTN_WIDTH),
            pl.BlockSpec((None, N_MEM, D_MODEL), lambda b, i: (b, 0, 0)),
            full((1, D_MODEL)), full((D_MODEL, 2 * XATTN_WIDTH)), full((MIX_WIDTH, D_MODEL)),
            full((1, D_MODEL)),
        ],
        out_specs=rows(D_MODEL),
        out_shape=jax.ShapeDtypeStruct((tokens, D_MODEL), F32),
        scratch_shapes=[pltpu.VMEM((N_MEM, XATTN_WIDTH), BF16), pltpu.VMEM((N_MEM, XATTN_WIDTH), BF16)],
        compiler_params=pltpu.CompilerParams(
            dimension_semantics=("arbitrary", "arbitrary"), vmem_limit_bytes=VMEM_LIMIT_BYTES),
        name="out_projection",
    )(x2, y_m, y_a, px, mem, g_mem, wkv_b, wout_b, g_final)


def _block_diag_dense(w_blk):
    rows = w_blk.reshape(MLSTM_HEADS, MLSTM_HEAD_DIM, QKV_BLOCK)
    tiled = jnp.tile(rows, (1, 1, MLSTM_HEAD_DIM // QKV_BLOCK))
    idx = jnp.arange(MLSTM_HEAD_DIM) // QKV_BLOCK
    same_block = idx[:, None] == idx[None, :]
    return jnp.where(same_block[None], tiled, 0.0).astype(BF16)


def _rope_tables():
    half = ROPE_DIM // 2
    pos = jnp.arange(SEQ, dtype=F32)
    inv = ROPE_THETA ** (-jnp.arange(0, ROPE_DIM, 2, dtype=F32) / ROPE_DIM)
    ang = pos[:, None] * inv[None, :]
    cos, sin = jnp.cos(ang), jnp.sin(ang)
    ones = jnp.ones((SEQ, ATTN_HEAD_DIM - ROPE_DIM), F32)
    zeros = jnp.zeros((SEQ, ATTN_HEAD_DIM - ROPE_DIM), F32)
    zero_h = jnp.zeros((SEQ, half), F32)
    a = jnp.concatenate([cos, cos, ones], axis=1)
    b = jnp.concatenate([-sin, zero_h, zeros], axis=1)
    c = jnp.concatenate([zero_h, sin, zeros], axis=1)
    tile = lambda t: jnp.concatenate([t, t], axis=1)
    return tile(a), tile(b), tile(c)


def kernel(x, mem, g_norm, w_in, conv_w, conv_b, w_q_blk, w_k_blk, w_v_blk, w_gate, b_gate,
           g_head, skip, g_mem, w_mem_kv, w_out, g_final):
    batch = x.shape[0]
    assert x.shape[1:] == (SEQ, D_MODEL) and g_norm.shape[0] == 1
    layer = 0
    x2 = x.reshape(batch * SEQ, D_MODEL)
    wg_pad = jnp.pad(w_gate[layer], ((0, 0), (0, LANES - 2 * MLSTM_HEADS))).astype(BF16)
    bg_pad = jnp.pad(b_gate[layer], (0, LANES - 2 * MLSTM_HEADS))[None, :]
    qa, ka, va, za, px, y_m = _inproj_mlstm(
        x2, g_norm[layer][None, :], w_in[layer].astype(BF16), conv_w[layer], conv_b[layer][None, :],
        _block_diag_dense(w_q_blk[layer]), _block_diag_dense(w_k_blk[layer]),
        _block_diag_dense(w_v_blk[layer]), wg_pad, bg_pad,
        g_head[layer][None, :], skip[layer][None, :], batch)

    rope_a, rope_b, rope_c = _rope_tables()
    y_a = _dilated_attention(qa, ka, va, za, rope_a, rope_b, rope_c, batch)

    out = _output_projection(x2, y_m, y_a, px, mem,
                             g_mem[layer][None, :], w_mem_kv[layer].astype(BF16),
                             w_out[layer].astype(BF16), g_final[None, :], batch)
    return out.reshape(batch, SEQ, D_MODEL)
```

```python
import jax
import jax.numpy as jnp
from jax import lax
from jax.experimental import pallas as pl
from jax.experimental.pallas import tpu as pltpu

F32 = jnp.float32
BF16 = jnp.bfloat16

D_MODEL = 1024
SEQ = 2048
N_MEM = 256
EPS = 1e-6

MLSTM_HEADS = 4
MLSTM_HEAD_DIM = 256
MLSTM_WIDTH = 1024
QKV_BLOCK = 4
CONV_WIDTH = 4

ATTN_HEADS = 8
ATTN_HEAD_DIM = 64
ATTN_WIDTH = 512
ATTN_PAIRS = ATTN_HEADS // 2
ROPE_DIM = 16
ROPE_THETA = 500000.0
BAND = 128
DIL_MID = 4
DIL_MAX = 16

XATTN_HEADS = 4
XATTN_HEAD_DIM = 128
XATTN_WIDTH = 512

MIX_WIDTH = MLSTM_WIDTH + ATTN_WIDTH + XATTN_WIDTH
IN_WIDTH = 3 * MLSTM_WIDTH + 4 * ATTN_WIDTH + 2 * XATTN_WIDTH

LANES = 128
VMEM_LIMIT_BYTES = 56 * 1024 * 1024
NEG = -1e30
LOG2E = 1.4426950408889634

ROW_TILE = 1024
PROJ_COLS = 512
MLSTM_CHUNK = 256


def _sigmoid(x):
    return 1.0 / (1.0 + jnp.exp2(x * (-LOG2E)))


def _silu(x):
    return x * _sigmoid(x)


def _log_sigmoid(x):
    return jnp.minimum(x, 0.0) - jnp.log(1.0 + jnp.exp(-jnp.abs(x)))


def _dot(a, b):
    return jnp.dot(a, b, preferred_element_type=F32)


def _dot_nt(a, b):
    return lax.dot_general(a, b, (((1,), (1,)), ((), ())), preferred_element_type=F32)


def _dot_tn(a, b):
    return lax.dot_general(a, b, (((0,), (0,)), ((), ())), preferred_element_type=F32)


def _mlstm_front(xm_ref, convw_ref, convb_ref, wq_ref, wk_ref, wv_ref, wg_ref, bg_ref, xbuf, stage):
    L = MLSTM_CHUNK
    dh = MLSTM_HEAD_DIM
    xc_ref, q_ref, k_ref, v_ref, gcol_ref, grow_ref = stage

    xbuf[8:8 + L, :] = xm_ref[...].astype(F32)
    conv = convb_ref[...] + convw_ref[3:4, :] * xbuf[8:8 + L, :]
    conv = conv + convw_ref[2:3, :] * xbuf[7:7 + L, :]
    conv = conv + convw_ref[1:2, :] * xbuf[6:6 + L, :]
    conv = conv + convw_ref[0:1, :] * xbuf[5:5 + L, :]
    xbuf[0:8, :] = xbuf[L:L + 8, :]
    xc = _silu(conv)
    xc_ref[...] = xc
    xc_b = xc.astype(BF16)
    yield

    k_scale = dh ** -0.5
    q_b, k_b, v_b = [], [], []
    for h in range(MLSTM_HEADS):
        sl = slice(h * dh, (h + 1) * dh)
        q_b.append(_dot(xc_b[:, sl], wq_ref[h]).astype(BF16))
        k = _dot(xc_b[:, sl], wk_ref[h])
        k_ref[:, sl] = k * k_scale
        k_b.append(k.astype(BF16))
        v_b.append(_dot(xm_ref[:, sl], wv_ref[h]).astype(BF16))
        q_ref[:, sl] = q_b[h]
        v_ref[:, sl] = v_b[h]
        yield

    qkv_b = jnp.concatenate(q_b + k_b + v_b, axis=1)
    gates = _dot(qkv_b, wg_ref[...]) + bg_ref[...]
    causal = lax.broadcasted_iota(jnp.int32, (L, L), 0) >= lax.broadcasted_iota(jnp.int32, (L, L), 1)
    b2 = jnp.dot(causal.astype(F32), _log_sigmoid(gates) * LOG2E, precision=lax.Precision.HIGHEST,
                 preferred_element_type=F32)
    w2 = gates * LOG2E - pltpu.roll(b2, LANES - MLSTM_HEADS, 1)
    lane = lax.broadcasted_iota(jnp.int32, (L, LANES), 1)
    gcol = jnp.where(lane < MLSTM_HEADS, w2, b2)
    gcol_ref[...] = gcol
    grow_ref[...] = gcol.T


def _mlstm_back(stage, zm_ref, om_ref, ghead_ref, skip_ref, y_ref, cn_scr, m_scr):
    L = MLSTM_CHUNK
    dh = MLSTM_HEAD_DIM
    xc_ref, q_ref, k_ref, v_ref, gcol_ref, grow_ref = stage
    causal = lax.broadcasted_iota(jnp.int32, (L, L), 0) >= lax.broadcasted_iota(jnp.int32, (L, L), 1)
    ones_b = jnp.ones((L, LANES), BF16)

    for h in range(MLSTM_HEADS):
        sl = slice(h * dh, (h + 1) * dh)
        w_col = gcol_ref[:, h:h + 1]
        b_col = gcol_ref[:, MLSTM_HEADS + h:MLSTM_HEADS + h + 1]
        w_row = grow_ref[h:h + 1, :]
        m_prev = m_scr[h, 0:1, 0:1]
        cn_old = cn_scr[h]
        q_b = q_ref[:, sl]
        k_f = k_ref[:, sl]
        v_aug = jnp.concatenate([v_ref[:, sl], ones_b], axis=1)

        mw = jnp.where(causal, w_row, NEG)
        mm = jnp.maximum(m_prev, jnp.max(mw, axis=1, keepdims=True))
        wts = (jnp.exp2(mw - mm) * _dot_nt(q_b, k_f.astype(BF16))).astype(BF16)
        inter = jnp.broadcast_to(jnp.exp2(m_prev - mm), (L, LANES))
        inter_c = _dot(q_b, cn_old.astype(BF16))
        intra = _dot(wts, v_aug)
        den = inter * inter_c[:, dh:] + intra[:, dh:]
        floor = jnp.broadcast_to(jnp.exp2(-(b_col + mm)), (L, LANES))
        r = 1.0 / jnp.maximum(jnp.abs(den), floor)
        h_t = (jnp.concatenate([inter, inter], axis=1) * inter_c[:, :dh] + intra[:, :dh]) * \
            jnp.concatenate([r, r], axis=1)

        mm_last = mm[L - 1:L, :]
        gk = (jnp.exp2(w_col - mm_last) * k_f).astype(BF16)
        decay = jnp.exp2(m_prev - mm_last)
        cn_scr[h] = decay * cn_old + _dot_tn(gk, v_aug)
        m_scr[h] = jnp.broadcast_to(b_col[L - 1:L, :] + mm_last, (8, LANES))
        yield

        hg = _sigmoid(om_ref[:, sl].astype(F32)) * h_t
        mu = jnp.mean(hg, axis=1, keepdims=True)
        cen = hg - mu
        var = jnp.mean(cen * cen, axis=1, keepdims=True)
        ln = cen * lax.rsqrt(var + EPS) * ghead_ref[:, sl]
        y = (ln + skip_ref[:, sl] * xc_ref[:, sl]) * _silu(zm_ref[:, sl].astype(F32))
        y_ref[:, sl] = y.astype(BF16)
        yield


def _round_robin(*gens):
    done = object()
    live = list(gens)
    while live:
        live = [g for g in live if next(g, done) is not done]


def _inproj_pieces(h_scr, w_ref, sinks):
    for col, store in sinks:
        store(_dot(h_scr[...], w_ref[:, col:col + PROJ_COLS]).astype(BF16))
        yield


def _inproj_mlstm_kernel(x_ref, g_ref, w_ref, convw_ref, convb_ref, wq_ref, wk_ref, wv_ref,
                         wg_ref, bg_ref, ghead_ref, skip_ref,
                         qa_ref, ka_ref, va_ref, za_ref, px_ref, ym_ref,
                         h_scr, pm_cur, xm_prev, zo_prev, xbuf, cn_scr, m_scr, *stages):
    L = MLSTM_CHUNK
    steps_per_seq = SEQ // (2 * L)
    j = pl.program_id(0)
    stage0, stage1 = stages[:len(stages) // 2], stages[len(stages) // 2:]
    front_args = (convw_ref, convb_ref, wq_ref, wk_ref, wv_ref, wg_ref, bg_ref, xbuf)
    back_args = (ghead_ref, skip_ref)

    @pl.when(j == 0)
    def _():
        xbuf[0:8, :] = jnp.zeros((8, MLSTM_WIDTH), F32)
        xm_prev[...] = jnp.zeros_like(xm_prev)
        zo_prev[...] = jnp.zeros_like(zo_prev)
        for ref in stage0:
            ref[...] = jnp.zeros_like(ref)

    @pl.when((j == 0) | (j % steps_per_seq == 1))
    def _():
        cn_scr[...] = jnp.zeros_like(cn_scr)
        m_scr[...] = jnp.zeros_like(m_scr)

    x = x_ref[...]
    ms = jnp.mean(x * x, axis=-1, keepdims=True)
    h_scr[...] = (x * lax.rsqrt(ms + EPS) * g_ref[...]).astype(BF16)

    def pm_sink(col):
        def store(r):
            pm_cur[:, col:col + PROJ_COLS] = r
        return store

    def pair_sink(ref, col):
        def store(r):
            for p in range(PROJ_COLS // LANES):
                ref[col // LANES + p] = r[:, p * LANES:(p + 1) * LANES]
        return store

    def px_sink(col):
        def store(r):
            px_ref[:, col:col + PROJ_COLS] = r
        return store

    base_a = 3 * MLSTM_WIDTH
    base_x = base_a + 4 * ATTN_WIDTH
    sinks_a = [(n * PROJ_COLS, pm_sink(n * PROJ_COLS)) for n in range(3 * MLSTM_WIDTH // PROJ_COLS)]
    sinks_b = [(base_a + k * ATTN_WIDTH + col, pair_sink(ref, col))
               for k, ref in enumerate((qa_ref, ka_ref, va_ref, za_ref))
               for col in range(0, ATTN_WIDTH, PROJ_COLS)]
    sinks_b += [(base_x + n * PROJ_COLS, px_sink(n * PROJ_COLS))
                for n in range(2 * XATTN_WIDTH // PROJ_COLS)]

    first, second = pl.ds(0, L), pl.ds(L, L)
    zm_cols, om_cols = pl.ds(0, MLSTM_WIDTH), pl.ds(MLSTM_WIDTH, MLSTM_WIDTH)
    _round_robin(
        _inproj_pieces(h_scr, w_ref, sinks_a),
        _mlstm_front(xm_prev, *front_args, stage1),
        _mlstm_back(stage0, zo_prev.at[first, zm_cols], zo_prev.at[first, om_cols], *back_args,
                    ym_ref.at[first], cn_scr, m_scr))

    @pl.when(j % steps_per_seq == 0)
    def _():
        xbuf[0:8, :] = jnp.zeros((8, MLSTM_WIDTH), F32)

    _round_robin(
        _inproj_pieces(h_scr, w_ref, sinks_b),
        _mlstm_front(pm_cur.at[first, pl.ds(0, MLSTM_WIDTH)], *front_args, stage0),
        _mlstm_back(stage1, zo_prev.at[second, zm_cols], zo_prev.at[second, om_cols], *back_args,
                    ym_ref.at[second], cn_scr, m_scr))

    xm_prev[...] = pm_cur[L:2 * L, 0:MLSTM_WIDTH]
    zo_prev[...] = pm_cur[:, MLSTM_WIDTH:3 * MLSTM_WIDTH]


def _inproj_mlstm(x2, g_norm, w_in_b, conv_w, conv_b, wq_d, wk_d, wv_d, wg_pad, bg_pad, g_head, skip,
                  batch):
    L = MLSTM_CHUNK
    tile = 2 * L
    tokens = x2.shape[0]
    tiles = tokens // tile
    tiles_per_seq = SEQ // tile
    assert SEQ % tile == 0
    dh = MLSTM_HEAD_DIM
    cur = lambda j: jnp.minimum(j, tiles - 1)
    full = lambda shape: pl.BlockSpec(shape, lambda j: (0,) * len(shape))
    pair_spec = pl.BlockSpec((None, ATTN_PAIRS, tile, LANES),
                             lambda j: (cur(j) // tiles_per_seq, 0, cur(j) % tiles_per_seq, 0))
    pair_shape = jax.ShapeDtypeStruct((batch, ATTN_PAIRS, SEQ, LANES), BF16)
    stage = [pltpu.VMEM((L, MLSTM_WIDTH), F32), pltpu.VMEM((L, MLSTM_WIDTH), BF16),
             pltpu.VMEM((L, MLSTM_WIDTH), F32), pltpu.VMEM((L, MLSTM_WIDTH), BF16),
             pltpu.VMEM((L, LANES), F32), pltpu.VMEM((LANES, L), F32)]
    return pl.pallas_call(
        _inproj_mlstm_kernel,
        grid=(tiles + 1,),
        in_specs=[
            pl.BlockSpec((tile, D_MODEL), lambda j: (cur(j), 0)),
            full((1, D_MODEL)),
            pl.BlockSpec((D_MODEL, IN_WIDTH), lambda j: (0, 0), pipeline_mode=pl.Buffered(1)),
            full((CONV_WIDTH, MLSTM_WIDTH)), full((1, MLSTM_WIDTH)),
            full((MLSTM_HEADS, dh, dh)), full((MLSTM_HEADS, dh, dh)), full((MLSTM_HEADS, dh, dh)),
            full((3 * MLSTM_WIDTH, LANES)), full((1, LANES)),
            full((1, MLSTM_WIDTH)), full((1, MLSTM_WIDTH)),
        ],
        out_specs=[
            pair_spec, pair_spec, pair_spec, pair_spec,
            pl.BlockSpec((tile, 2 * XATTN_WIDTH), lambda j: (cur(j), 0)),
            pl.BlockSpec((tile, MLSTM_WIDTH), lambda j: (jnp.maximum(j - 1, 0), 0)),
        ],
        out_shape=[
            pair_shape, pair_shape, pair_shape, pair_shape,
            jax.ShapeDtypeStruct((tokens, 2 * XATTN_WIDTH), BF16),
            jax.ShapeDtypeStruct((tokens, MLSTM_WIDTH), BF16),
        ],
        scratch_shapes=[
            pltpu.VMEM((tile, D_MODEL), BF16),
            pltpu.VMEM((tile, 3 * MLSTM_WIDTH), BF16),
            pltpu.VMEM((L, MLSTM_WIDTH), BF16),
            pltpu.VMEM((tile, 2 * MLSTM_WIDTH), BF16),
            pltpu.VMEM((L + 8, MLSTM_WIDTH), F32),
            pltpu.VMEM((MLSTM_HEADS, dh, dh + LANES), F32),
            pltpu.VMEM((MLSTM_HEADS, 8, LANES), F32),
        ] + stage + stage,
        compiler_params=pltpu.CompilerParams(
            dimension_semantics=("arbitrary",), vmem_limit_bytes=VMEM_LIMIT_BYTES),
        name="in_projection_mlstm",
    )(x2, g_norm, w_in_b, conv_w, conv_b, wq_d, wk_d, wv_d, wg_pad, bg_pad, g_head, skip)


def _band_scores(q_top, q_bot, kk, bias):
    q2 = jnp.concatenate([q_top, q_bot], axis=0).astype(BF16)
    return _dot_nt(q2, kk.astype(BF16)) + jnp.concatenate([bias, bias], axis=0)


def _band_output(s, vv):
    nk = vv.shape[0]
    m = jnp.max(s, axis=1, keepdims=True)
    p = jnp.exp2(s - m).astype(BF16)
    v_aug = jnp.concatenate([vv.astype(BF16), jnp.ones((nk, LANES), BF16)], axis=1)
    o2 = _dot(p, v_aug)
    first = lax.broadcasted_iota(jnp.int32, (BAND, LANES), 1) < ATTN_HEAD_DIM
    o = jnp.where(first, o2[:BAND, :LANES], o2[BAND:, :LANES])
    l_p = jnp.where(first, o2[:BAND, LANES:], o2[BAND:, LANES:])
    m_p = jnp.where(first, m[:BAND], m[BAND:])
    return o, m_p, l_p


def _merge(a, b):
    (o_a, m_a, l_a), (o_b, m_b, l_b) = a, b
    m_new = jnp.maximum(m_a, m_b)
    w_a = jnp.exp2(m_a - m_new)
    w_b = jnp.exp2(m_b - m_new)
    return o_a * w_a + o_b * w_b, m_new, l_a * w_a + l_b * w_b


def _dilated_kernel(q_ref, k_ref, v_ref, z_ref, ra_ref, rb_ref, rc_ref, out_ref,
                    q0, q1, ks, vs, q40, q41, k4, v4, a1o, a1m, a1l, a4o, a4m, a4l, zbuf,
                    bias_prev, bias_self):
    nblk = SEQ // BAND
    span = DIL_MID * BAND
    q_scale = ATTN_HEAD_DIM ** -0.5 * LOG2E

    t = lax.broadcasted_iota(jnp.int32, (BAND, 2 * BAND), 0)
    c = lax.broadcasted_iota(jnp.int32, (BAND, 2 * BAND), 1)
    bias_prev[...] = jnp.where((c >= t) & (c <= t + BAND), 0.0, NEG)
    t = lax.broadcasted_iota(jnp.int32, (BAND, BAND), 0)
    c = lax.broadcasted_iota(jnp.int32, (BAND, BAND), 1)
    bias_self[...] = jnp.where(c <= t, 0.0, NEG)

    def rope(x_b, rows):
        partner = _dot(x_b, rc_ref[...])
        return x_b.astype(F32) * ra_ref[rows, :] + partner * rb_ref[rows, :]

    def prep(i, carry):
        rows = pl.ds(pl.multiple_of(i * BAND, BAND), BAND)
        q = rope(q_ref[rows, :], rows) * q_scale
        first = lax.broadcasted_iota(jnp.int32, (BAND, LANES), 1) < ATTN_HEAD_DIM
        q0[rows, :] = jnp.where(first, q, 0.0)
        q1[rows, :] = jnp.where(first, 0.0, q)
        ks[rows, :] = rope(k_ref[rows, :], rows)
        vs[rows, :] = v_ref[rows, :].astype(F32)
        zbuf[rows, :] = _silu(z_ref[rows, :].astype(F32))
        return carry

    lax.fori_loop(0, nblk, prep, 0, unroll=8)

    def deinterleave(idx, carry):
        src = pl.ds(idx // DIL_MID + (idx % DIL_MID) * span, BAND, stride=DIL_MID)
        dst = pl.ds(pl.multiple_of(idx * BAND, BAND), BAND)
        q40[dst, :] = q0[src, :]
        q41[dst, :] = q1[src, :]
        k4[dst, :] = ks[src, :]
        v4[dst, :] = vs[src, :]
        return carry

    lax.fori_loop(0, nblk, deinterleave, 0, unroll=True)

    def put(acc, rows, res):
        for ref, val in zip(acc, res):
            ref[rows, :] = val

    acc1 = (a1o, a1m, a1l)
    acc4 = (a4o, a4m, a4l)

    tasks = []
    for qa, qb, kbuf, vbuf, acc, per_class in ((q0, q1, ks, vs, acc1, nblk),
                                                (q40, q41, k4, v4, acc4, nblk // DIL_MID)):
        for b in range(nblk):
            rows = pl.ds(b * BAND, BAND)
            win = None if b % per_class == 0 else pl.ds((b - 1) * BAND, 2 * BAND)
            tasks.append((qa, qb, kbuf, vbuf, rows, win, acc, False))
    for cls in range(DIL_MID):
        for r in range(DIL_MID):
            rows = pl.ds(cls * (SEQ // DIL_MID) + r, BAND, stride=DIL_MID)
            tasks.append((q40, q41, k4, v4, rows, None, acc4, True))

    def scores(task):
        qa, qb, kbuf, _, rows, win, _, _ = task
        if win is None:
            return _band_scores(qa[rows, :], qb[rows, :], kbuf[rows, :], bias_self[...])
        return _band_scores(qa[rows, :], qb[rows, :], kbuf[win, :], bias_prev[...])

    def complete(task, s):
        _, _, _, vbuf, rows, win, acc, merge = task
        res = _band_output(s, vbuf[rows if win is None else win, :])
        if merge:
            res = _merge(tuple(ref[rows, :] for ref in acc), res)
        put(acc, rows, res)

    lookahead = 4
    pending = []
    for task in tasks:
        pending.append((task, scores(task)))
        if len(pending) > lookahead:
            complete(*pending.pop(0))
    for item in pending:
        complete(*item)

    for b in range(nblk):
        rows4 = pl.ds(b * BAND, BAND)
        rows1 = pl.ds(b // DIL_MID + (b % DIL_MID) * span, BAND, stride=DIL_MID)
        o, _, l = _merge(tuple(ref[rows1, :] for ref in acc1), tuple(ref[rows4, :] for ref in acc4))
        zbuf[rows1, :] = o * (1.0 / l) * zbuf[rows1, :]

    def finish(i, carry):
        rows = pl.ds(pl.multiple_of(i * BAND, BAND), BAND)
        out_ref[rows, :] = zbuf[rows, :].astype(BF16)
        return carry

    lax.fori_loop(0, nblk, finish, 0, unroll=4)


def _dilated_attention(qa, ka, va, za, rope_a, rope_b, rope_c, batch):
    slab = pl.BlockSpec((None, None, SEQ, LANES), lambda b, j: (b, j, 0, 0))
    table = pl.BlockSpec((SEQ, LANES), lambda b, j: (0, 0))
    return pl.pallas_call(
        _dilated_kernel,
        grid=(batch, ATTN_PAIRS),
        in_specs=[slab, slab, slab, slab, table, table,
                  pl.BlockSpec((LANES, LANES), lambda b, j: (0, 0))],
        out_specs=slab,
        out_shape=jax.ShapeDtypeStruct((batch, ATTN_PAIRS, SEQ, LANES), BF16),
        scratch_shapes=[pltpu.VMEM((SEQ, LANES), F32) for _ in range(15)] + [
            pltpu.VMEM((BAND, 2 * BAND), F32), pltpu.VMEM((BAND, BAND), F32)],
        compiler_params=pltpu.CompilerParams(
            dimension_semantics=("arbitrary", "arbitrary"), vmem_limit_bytes=VMEM_LIMIT_BYTES),
        name="dilated_attention",
    )(qa, ka, va, za, rope_a, rope_b, rope_c)


def _outproj_kernel(x_ref, ym_ref, ya_ref, px_ref, mem_ref, gmem_ref, wkv_ref, wout_ref,
                    gfin_ref, out_ref, kx_scr, vx_scr):
    @pl.when(pl.program_id(1) == 0)
    def _():
        mem = mem_ref[...]
        ms = jnp.mean(mem * mem, axis=-1, keepdims=True)
        mem_n = (mem * lax.rsqrt(ms + EPS) * gmem_ref[...]).astype(BF16)
        kv = _dot(mem_n, wkv_ref[...])
        kx_scr[...] = kv[:, :XATTN_WIDTH].astype(BF16)
        vx_scr[...] = kv[:, XATTN_WIDTH:].astype(BF16)

    scale = XATTN_HEAD_DIM ** -0.5 * LOG2E
    ones_b = jnp.ones((N_MEM, LANES), BF16)
    y_a = jnp.concatenate([ya_ref[j] for j in range(ATTN_PAIRS)], axis=1)
    y = _dot(y_a, wout_ref[MLSTM_WIDTH:MLSTM_WIDTH + ATTN_WIDTH, :])
    heads = []
    for h in range(XATTN_HEADS):
        sl = slice(h * XATTN_HEAD_DIM, (h + 1) * XATTN_HEAD_DIM)
        s = _dot_nt(px_ref[:, sl], kx_scr[:, sl]) * scale
        m = jnp.max(s, axis=1, keepdims=True)
        p = jnp.exp2(s - m).astype(BF16)
        o2 = _dot(p, jnp.concatenate([vx_scr[:, sl], ones_b], axis=1))
        heads.append(o2[:, :LANES] * (1.0 / o2[:, LANES:]))
        rows_m = slice(h * MLSTM_HEAD_DIM, (h + 1) * MLSTM_HEAD_DIM)
        y = y + _dot(ym_ref[:, rows_m], wout_ref[rows_m, :])
    o_x = jnp.concatenate(heads, axis=1)
    y_x = (o_x * _silu(px_ref[:, XATTN_WIDTH:].astype(F32))).astype(BF16)
    y = y + _dot(y_x, wout_ref[MLSTM_WIDTH + ATTN_WIDTH:, :])
    r = x_ref[...] + y
    ms = jnp.mean(r * r, axis=-1, keepdims=True)
    out_ref[...] = r * lax.rsqrt(ms + EPS) * gfin_ref[...]


def _output_projection(x2, y_m, y_a, px, mem, g_mem, wkv_b, wout_b, g_final, batch):
    tokens = x2.shape[0]
    tps = SEQ // ROW_TILE
    rows = lambda width: pl.BlockSpec((ROW_TILE, width), lambda b, i: (b * tps + i, 0))
    full = lambda shape: pl.BlockSpec(shape, lambda b, i: (0,) * len(shape))
    return pl.pallas_call(
        _outproj_kernel,
        grid=(batch, tps),
        in_specs=[
            rows(D_MODEL), rows(MLSTM_WIDTH),
            pl.BlockSpec((None, ATTN_PAIRS, ROW_TILE, LANES), lambda b, i: (b, 0, i, 0)),
            rows(2 * XATTN_WIDTH),
            pl.BlockSpec((None, N_MEM, D_MODEL), lambda b, i: (b, 0, 0)),
            full((1, D_MODEL)), full((D_MODEL, 2 * XATTN_WIDTH)), full((MIX_WIDTH, D_MODEL)),
            full((1, D_MODEL)),
        ],
        out_specs=rows(D_MODEL),
        out_shape=jax.ShapeDtypeStruct((tokens, D_MODEL), F32),
        scratch_shapes=[pltpu.VMEM((N_MEM, XATTN_WIDTH), BF16), pltpu.VMEM((N_MEM, XATTN_WIDTH), BF16)],
        compiler_params=pltpu.CompilerParams(
            dimension_semantics=("arbitrary", "arbitrary"), vmem_limit_bytes=VMEM_LIMIT_BYTES),
        name="out_projection",
    )(x2, y_m, y_a, px, mem, g_mem, wkv_b, wout_b, g_final)


def _block_diag_dense(w_blk):
    rows = w_blk.reshape(MLSTM_HEADS, MLSTM_HEAD_DIM, QKV_BLOCK)
    tiled = jnp.tile(rows, (1, 1, MLSTM_HEAD_DIM // QKV_BLOCK))
    idx = jnp.arange(MLSTM_HEAD_DIM) // QKV_BLOCK
    same_block = idx[:, None] == idx[None, :]
    return jnp.where(same_block[None], tiled, 0.0).astype(BF16)


def _rope_tables():
    half = ROPE_DIM // 2
    pos = jnp.arange(SEQ, dtype=F32)
    inv = ROPE_THETA ** (-jnp.arange(0, ROPE_DIM, 2, dtype=F32) / ROPE_DIM)
    ang = pos[:, None] * inv[None, :]
    cos, sin = jnp.cos(ang), jnp.sin(ang)
    ones = jnp.ones((SEQ, ATTN_HEAD_DIM - ROPE_DIM), F32)
    zeros = jnp.zeros((SEQ, ATTN_HEAD_DIM - ROPE_DIM), F32)
    zero_h = jnp.zeros((SEQ, half), F32)
    a = jnp.concatenate([cos, cos, ones], axis=1)
    d = jnp.concatenate([-sin, sin, zeros], axis=1)
    tile = lambda t: jnp.concatenate([t, t], axis=1)
    lane = jnp.arange(LANES)
    in_head = lane % ATTN_HEAD_DIM
    source = jnp.where(in_head < half, lane + half, lane - half)
    perm = (lane[:, None] == source[None, :]) & (in_head < ROPE_DIM)[None, :]
    return tile(a), tile(d), perm.astype(BF16)


def kernel(x, mem, g_norm, w_in, conv_w, conv_b, w_q_blk, w_k_blk, w_v_blk, w_gate, b_gate,
           g_head, skip, g_mem, w_mem_kv, w_out, g_final):
    batch = x.shape[0]
    assert x.shape[1:] == (SEQ, D_MODEL) and g_norm.shape[0] == 1
    layer = 0
    x2 = x.reshape(batch * SEQ, D_MODEL)
    wg_pad = jnp.pad(w_gate[layer], ((0, 0), (0, LANES - 2 * MLSTM_HEADS))).astype(BF16)
    bg_pad = jnp.pad(b_gate[layer], (0, LANES - 2 * MLSTM_HEADS))[None, :]
    qa, ka, va, za, px, y_m = _inproj_mlstm(
        x2, g_norm[layer][None, :], w_in[layer].astype(BF16), conv_w[layer], conv_b[layer][None, :],
        _block_diag_dense(w_q_blk[layer]), _block_diag_dense(w_k_blk[layer]),
        _block_diag_dense(w_v_blk[layer]), wg_pad, bg_pad,
        g_head[layer][None, :], skip[layer][None, :], batch)

    rope_a, rope_b, rope_c = _rope_tables()
    y_a = _dilated_attention(qa, ka, va, za, rope_a, rope_b, rope_c, batch)

    out = _output_projection(x2, y_m, y_a, px, mem,
                             g_mem[layer][None, :], w_mem_kv[layer].astype(BF16),
                             w_out[layer].astype(BF16), g_final[None, :], batch)
    return out.reshape(batch, SEQ, D_MODEL)
```

```python
import jax
import jax.numpy as jnp
from jax import lax
from jax.experimental import pallas as pl
from jax.experimental.pallas import tpu as pltpu

F32 = jnp.float32
BF16 = jnp.bfloat16

D_MODEL = 1024
SEQ = 2048
N_MEM = 256
EPS = 1e-6

MLSTM_HEADS = 4
MLSTM_HEAD_DIM = 256
MLSTM_WIDTH = 1024
QKV_BLOCK = 4
CONV_WIDTH = 4

ATTN_HEADS = 8
ATTN_HEAD_DIM = 64
ATTN_WIDTH = 512
ATTN_PAIRS = ATTN_HEADS // 2
ROPE_DIM = 16
ROPE_THETA = 500000.0
BAND = 128
DIL_MID = 4
DIL_MAX = 16

XATTN_HEADS = 4
XATTN_HEAD_DIM = 128
XATTN_WIDTH = 512

MIX_WIDTH = MLSTM_WIDTH + ATTN_WIDTH + XATTN_WIDTH
IN_WIDTH = 3 * MLSTM_WIDTH + 4 * ATTN_WIDTH + 2 * XATTN_WIDTH

LANES = 128
VMEM_LIMIT_BYTES = 56 * 1024 * 1024
NEG = -1e30
LOG2E = 1.4426950408889634

ROW_TILE = 1024
PROJ_COLS = 512
MLSTM_CHUNK = 256


def _sigmoid(x):
    return 1.0 / (1.0 + jnp.exp2(x * (-LOG2E)))


def _silu(x):
    return x * _sigmoid(x)


def _log_sigmoid(x):
    return jnp.minimum(x, 0.0) - jnp.log(1.0 + jnp.exp(-jnp.abs(x)))


def _dot(a, b):
    return jnp.dot(a, b, preferred_element_type=F32)


def _dot_nt(a, b):
    return lax.dot_general(a, b, (((1,), (1,)), ((), ())), preferred_element_type=F32)


def _dot_tn(a, b):
    return lax.dot_general(a, b, (((0,), (0,)), ((), ())), preferred_element_type=F32)


def _mlstm_front(xm_ref, convw_ref, convb_ref, shift_ref, wq_ref, wk_ref, wv_ref, wg_ref, bg_ref,
                 xbuf, stage):
    L = MLSTM_CHUNK
    dh = MLSTM_HEAD_DIM
    xc_ref, q_ref, k_ref, v_ref, gcol_ref, grow_ref = stage

    xm_b = xm_ref[...]
    xm = xm_b.astype(F32)
    conv = convb_ref[...] + convw_ref[3:4, :] * xm
    head = jnp.zeros((8, MLSTM_WIDTH), F32)
    for shift in range(1, CONV_WIDTH):
        w = convw_ref[CONV_WIDTH - 1 - shift:CONV_WIDTH - shift, :]
        conv = conv + w * _dot(shift_ref[shift - 1], xm_b)
        head = head + w * xbuf[8 - shift:16 - shift, :]
    conv = jnp.concatenate([conv[0:8] + head, conv[8:]], axis=0)
    xbuf[0:8, :] = xm[L - 8:L, :]
    xc = _silu(conv)
    xc_ref[...] = xc
    xc_b = xc.astype(BF16)
    yield

    k_scale = dh ** -0.5
    q_b, k_b, v_b = [], [], []
    for h in range(MLSTM_HEADS):
        sl = slice(h * dh, (h + 1) * dh)
        q_b.append(_dot(xc_b[:, sl], wq_ref[h]).astype(BF16))
        k = _dot(xc_b[:, sl], wk_ref[h])
        k_ref[:, sl] = k * k_scale
        k_b.append(k.astype(BF16))
        v_b.append(_dot(xm_ref[:, sl], wv_ref[h]).astype(BF16))
        q_ref[:, sl] = q_b[h]
        v_ref[:, sl] = v_b[h]
        yield

    qkv_b = jnp.concatenate(q_b + k_b + v_b, axis=1)
    gates = _dot(qkv_b, wg_ref[...]) + bg_ref[...]
    causal = lax.broadcasted_iota(jnp.int32, (L, L), 0) >= lax.broadcasted_iota(jnp.int32, (L, L), 1)
    b2 = jnp.dot(causal.astype(F32), _log_sigmoid(gates) * LOG2E, precision=lax.Precision.HIGHEST,
                 preferred_element_type=F32)
    w2 = gates * LOG2E - pltpu.roll(b2, LANES - MLSTM_HEADS, 1)
    lane = lax.broadcasted_iota(jnp.int32, (L, LANES), 1)
    gcol = jnp.where(lane < MLSTM_HEADS, w2, b2)
    gcol_ref[...] = gcol
    grow_ref[...] = gcol.T


def _mlstm_back(stage, zm_ref, om_ref, ghead_ref, skip_ref, y_ref, cn_scr, m_scr):
    L = MLSTM_CHUNK
    dh = MLSTM_HEAD_DIM
    xc_ref, q_ref, k_ref, v_ref, gcol_ref, grow_ref = stage
    causal = lax.broadcasted_iota(jnp.int32, (L, L), 0) >= lax.broadcasted_iota(jnp.int32, (L, L), 1)
    ones_b = jnp.ones((L, LANES), BF16)

    for h in range(MLSTM_HEADS):
        sl = slice(h * dh, (h + 1) * dh)
        w_col = gcol_ref[:, h:h + 1]
        b_col = gcol_ref[:, MLSTM_HEADS + h:MLSTM_HEADS + h + 1]
        w_row = grow_ref[h:h + 1, :]
        m_prev = m_scr[h, 0:1, 0:1]
        cn_old = cn_scr[h]
        q_b = q_ref[:, sl]
        k_f = k_ref[:, sl]
        v_aug = jnp.concatenate([v_ref[:, sl], ones_b], axis=1)

        mw = jnp.where(causal, w_row, NEG)
        mm = jnp.maximum(m_prev, jnp.max(mw, axis=1, keepdims=True))
        wts = (jnp.exp2(mw - mm) * _dot_nt(q_b, k_f.astype(BF16))).astype(BF16)
        inter = jnp.broadcast_to(jnp.exp2(m_prev - mm), (L, LANES))
        inter_c = _dot(q_b, cn_old.astype(BF16))
        intra = _dot(wts, v_aug)
        den = inter * inter_c[:, dh:] + intra[:, dh:]
        floor = jnp.broadcast_to(jnp.exp2(-(b_col + mm)), (L, LANES))
        r = 1.0 / jnp.maximum(jnp.abs(den), floor)
        h_t = (jnp.concatenate([inter, inter], axis=1) * inter_c[:, :dh] + intra[:, :dh]) * \
            jnp.concatenate([r, r], axis=1)

        mm_last = mm[L - 1:L, :]
        gk = (jnp.exp2(w_col - mm_last) * k_f).astype(BF16)
        decay = jnp.exp2(m_prev - mm_last)
        cn_scr[h] = decay * cn_old + _dot_tn(gk, v_aug)
        m_scr[h] = jnp.broadcast_to(b_col[L - 1:L, :] + mm_last, (8, LANES))
        yield

        hg = _sigmoid(om_ref[:, sl].astype(F32)) * h_t
        mu = jnp.mean(hg, axis=1, keepdims=True)
        cen = hg - mu
        var = jnp.mean(cen * cen, axis=1, keepdims=True)
        ln = cen * lax.rsqrt(var + EPS) * ghead_ref[:, sl]
        y = (ln + skip_ref[:, sl] * xc_ref[:, sl]) * _silu(zm_ref[:, sl].astype(F32))
        y_ref[:, sl] = y.astype(BF16)
        yield


def _round_robin(*gens):
    done = object()
    live = list(gens)
    while live:
        live = [g for g in live if next(g, done) is not done]


def _inproj_pieces(h_scr, w_ref, sinks):
    for col, store in sinks:
        store(_dot(h_scr[...], w_ref[:, col:col + PROJ_COLS]).astype(BF16))
        yield


def _inproj_mlstm_kernel(x_ref, g_ref, w_ref, convw_ref, convb_ref, shift_ref, wq_ref, wk_ref, wv_ref,
                         wg_ref, bg_ref, ghead_ref, skip_ref,
                         qa_ref, ka_ref, va_ref, za_ref, px_ref, ym_ref,
                         h_scr, pm_cur, xm_prev, zo_prev, xbuf, cn_scr, m_scr, *stages):
    L = MLSTM_CHUNK
    steps_per_seq = SEQ // (2 * L)
    j = pl.program_id(0)
    stage0, stage1 = stages[:len(stages) // 2], stages[len(stages) // 2:]
    front_args = (convw_ref, convb_ref, shift_ref, wq_ref, wk_ref, wv_ref, wg_ref, bg_ref, xbuf)
    back_args = (ghead_ref, skip_ref)

    @pl.when(j == 0)
    def _():
        xbuf[...] = jnp.zeros_like(xbuf)
        xm_prev[...] = jnp.zeros_like(xm_prev)
        zo_prev[...] = jnp.zeros_like(zo_prev)
        for ref in stage0:
            ref[...] = jnp.zeros_like(ref)

    @pl.when((j == 0) | (j % steps_per_seq == 1))
    def _():
        cn_scr[...] = jnp.zeros_like(cn_scr)
        m_scr[...] = jnp.zeros_like(m_scr)

    x = x_ref[...]
    ms = jnp.mean(x * x, axis=-1, keepdims=True)
    h_scr[...] = (x * lax.rsqrt(ms + EPS) * g_ref[...]).astype(BF16)

    def pm_sink(col):
        def store(r):
            pm_cur[:, col:col + PROJ_COLS] = r
        return store

    def pair_sink(ref, col):
        def store(r):
            for p in range(PROJ_COLS // LANES):
                ref[col // LANES + p] = r[:, p * LANES:(p + 1) * LANES]
        return store

    def px_sink(col):
        def store(r):
            px_ref[:, col:col + PROJ_COLS] = r
        return store

    base_a = 3 * MLSTM_WIDTH
    base_x = base_a + 4 * ATTN_WIDTH
    sinks_a = [(n * PROJ_COLS, pm_sink(n * PROJ_COLS)) for n in range(3 * MLSTM_WIDTH // PROJ_COLS)]
    sinks_b = [(base_a + k * ATTN_WIDTH + col, pair_sink(ref, col))
               for k, ref in enumerate((qa_ref, ka_ref, va_ref, za_ref))
               for col in range(0, ATTN_WIDTH, PROJ_COLS)]
    sinks_b += [(base_x + n * PROJ_COLS, px_sink(n * PROJ_COLS))
                for n in range(2 * XATTN_WIDTH // PROJ_COLS)]

    first, second = pl.ds(0, L), pl.ds(L, L)
    zm_cols, om_cols = pl.ds(0, MLSTM_WIDTH), pl.ds(MLSTM_WIDTH, MLSTM_WIDTH)
    _round_robin(
        _inproj_pieces(h_scr, w_ref, sinks_a),
        _mlstm_front(xm_prev, *front_args, stage1),
        _mlstm_back(stage0, zo_prev.at[first, zm_cols], zo_prev.at[first, om_cols], *back_args,
                    ym_ref.at[first], cn_scr, m_scr))

    @pl.when(j % steps_per_seq == 0)
    def _():
        xbuf[0:8, :] = jnp.zeros((8, MLSTM_WIDTH), F32)

    _round_robin(
        _inproj_pieces(h_scr, w_ref, sinks_b),
        _mlstm_front(pm_cur.at[first, pl.ds(0, MLSTM_WIDTH)], *front_args, stage0),
        _mlstm_back(stage1, zo_prev.at[second, zm_cols], zo_prev.at[second, om_cols], *back_args,
                    ym_ref.at[second], cn_scr, m_scr))

    xm_prev[...] = pm_cur[L:2 * L, 0:MLSTM_WIDTH]
    zo_prev[...] = pm_cur[:, MLSTM_WIDTH:3 * MLSTM_WIDTH]


def _inproj_mlstm(x2, g_norm, w_in_b, conv_w, conv_b, wq_d, wk_d, wv_d, wg_pad, bg_pad, g_head, skip,
                  batch):
    L = MLSTM_CHUNK
    tile = 2 * L
    tokens = x2.shape[0]
    tiles = tokens // tile
    tiles_per_seq = SEQ // tile
    assert SEQ % tile == 0
    dh = MLSTM_HEAD_DIM
    cur = lambda j: jnp.minimum(j, tiles - 1)
    full = lambda shape: pl.BlockSpec(shape, lambda j: (0,) * len(shape))
    pair_spec = pl.BlockSpec((None, ATTN_PAIRS, tile, LANES),
                             lambda j: (cur(j) // tiles_per_seq, 0, cur(j) % tiles_per_seq, 0))
    pair_shape = jax.ShapeDtypeStruct((batch, ATTN_PAIRS, SEQ, LANES), BF16)
    stage = [pltpu.VMEM((L, MLSTM_WIDTH), F32), pltpu.VMEM((L, MLSTM_WIDTH), BF16),
             pltpu.VMEM((L, MLSTM_WIDTH), F32), pltpu.VMEM((L, MLSTM_WIDTH), BF16),
             pltpu.VMEM((L, LANES), F32), pltpu.VMEM((LANES, L), F32)]
    shifts = jnp.stack([jnp.eye(L, k=-s, dtype=BF16) for s in range(1, CONV_WIDTH)])
    return pl.pallas_call(
        _inproj_mlstm_kernel,
        grid=(tiles + 1,),
        in_specs=[
            pl.BlockSpec((tile, D_MODEL), lambda j: (cur(j), 0)),
            full((1, D_MODEL)),
            pl.BlockSpec((D_MODEL, IN_WIDTH), lambda j: (0, 0), pipeline_mode=pl.Buffered(1)),
            full((CONV_WIDTH, MLSTM_WIDTH)), full((1, MLSTM_WIDTH)), full((CONV_WIDTH - 1, L, L)),
            full((MLSTM_HEADS, dh, dh)), full((MLSTM_HEADS, dh, dh)), full((MLSTM_HEADS, dh, dh)),
            full((3 * MLSTM_WIDTH, LANES)), full((1, LANES)),
            full((1, MLSTM_WIDTH)), full((1, MLSTM_WIDTH)),
        ],
        out_specs=[
            pair_spec, pair_spec, pair_spec, pair_spec,
            pl.BlockSpec((tile, 2 * XATTN_WIDTH), lambda j: (cur(j), 0)),
            pl.BlockSpec((tile, MLSTM_WIDTH), lambda j: (jnp.maximum(j - 1, 0), 0)),
        ],
        out_shape=[
            pair_shape, pair_shape, pair_shape, pair_shape,
            jax.ShapeDtypeStruct((tokens, 2 * XATTN_WIDTH), BF16),
            jax.ShapeDtypeStruct((tokens, MLSTM_WIDTH), BF16),
        ],
        scratch_shapes=[
            pltpu.VMEM((tile, D_MODEL), BF16),
            pltpu.VMEM((tile, 3 * MLSTM_WIDTH), BF16),
            pltpu.VMEM((L, MLSTM_WIDTH), BF16),
            pltpu.VMEM((tile, 2 * MLSTM_WIDTH), BF16),
            pltpu.VMEM((16, MLSTM_WIDTH), F32),
            pltpu.VMEM((MLSTM_HEADS, dh, dh + LANES), F32),
            pltpu.VMEM((MLSTM_HEADS, 8, LANES), F32),
        ] + stage + stage,
        compiler_params=pltpu.CompilerParams(
            dimension_semantics=("arbitrary",), vmem_limit_bytes=VMEM_LIMIT_BYTES),
        name="in_projection_mlstm",
    )(x2, g_norm, w_in_b, conv_w, conv_b, shifts, wq_d, wk_d, wv_d, wg_pad, bg_pad, g_head, skip)


def _band_scores(q_top, q_bot, kk, bias):
    q2 = jnp.concatenate([q_top, q_bot], axis=0).astype(BF16)
    return _dot_nt(q2, kk.astype(BF16)) + jnp.concatenate([bias, bias], axis=0)


def _band_output(s, vv):
    nk = vv.shape[0]
    m = jnp.max(s, axis=1, keepdims=True)
    p = jnp.exp2(s - m).astype(BF16)
    v_aug = jnp.concatenate([vv.astype(BF16), jnp.ones((nk, LANES), BF16)], axis=1)
    o2 = _dot(p, v_aug)
    first = lax.broadcasted_iota(jnp.int32, (BAND, LANES), 1) < ATTN_HEAD_DIM
    o = jnp.where(first, o2[:BAND, :LANES], o2[BAND:, :LANES])
    l_p = jnp.where(first, o2[:BAND, LANES:], o2[BAND:, LANES:])
    m_p = jnp.where(first, m[:BAND], m[BAND:])
    return o, m_p, l_p


def _merge(a, b):
    (o_a, m_a, l_a), (o_b, m_b, l_b) = a, b
    m_new = jnp.maximum(m_a, m_b)
    w_a = jnp.exp2(m_a - m_new)
    w_b = jnp.exp2(m_b - m_new)
    return o_a * w_a + o_b * w_b, m_new, l_a * w_a + l_b * w_b


def _dilated_kernel(q_ref, k_ref, v_ref, z_ref, ra_ref, rb_ref, rc_ref, out_ref,
                    q0, q1, ks, vs, q40, q41, k4, v4, a1o, a1m, a1l, a4o, a4m, a4l, zbuf,
                    bias_prev, bias_self):
    nblk = SEQ // BAND
    span = DIL_MID * BAND
    q_scale = ATTN_HEAD_DIM ** -0.5 * LOG2E

    t = lax.broadcasted_iota(jnp.int32, (BAND, 2 * BAND), 0)
    c = lax.broadcasted_iota(jnp.int32, (BAND, 2 * BAND), 1)
    bias_prev[...] = jnp.where((c >= t) & (c <= t + BAND), 0.0, NEG)
    t = lax.broadcasted_iota(jnp.int32, (BAND, BAND), 0)
    c = lax.broadcasted_iota(jnp.int32, (BAND, BAND), 1)
    bias_self[...] = jnp.where(c <= t, 0.0, NEG)

    def rope(x_b, rows):
        partner = _dot(x_b, rc_ref[...])
        return x_b.astype(F32) * ra_ref[rows, :] + partner * rb_ref[rows, :]

    def prep(i, carry):
        rows = pl.ds(pl.multiple_of(i * BAND, BAND), BAND)
        q = rope(q_ref[rows, :], rows) * q_scale
        first = lax.broadcasted_iota(jnp.int32, (BAND, LANES), 1) < ATTN_HEAD_DIM
        q0[rows, :] = jnp.where(first, q, 0.0)
        q1[rows, :] = jnp.where(first, 0.0, q)
        ks[rows, :] = rope(k_ref[rows, :], rows)
        vs[rows, :] = v_ref[rows, :].astype(F32)
        zbuf[rows, :] = _silu(z_ref[rows, :].astype(F32))
        return carry

    lax.fori_loop(0, nblk, prep, 0, unroll=8)

    def deinterleave(idx, carry):
        src = pl.ds(idx // DIL_MID + (idx % DIL_MID) * span, BAND, stride=DIL_MID)
        dst = pl.ds(pl.multiple_of(idx * BAND, BAND), BAND)
        q40[dst, :] = q0[src, :]
        q41[dst, :] = q1[src, :]
        k4[dst, :] = ks[src, :]
        v4[dst, :] = vs[src, :]
        return carry

    lax.fori_loop(0, nblk, deinterleave, 0, unroll=True)

    def put(acc, rows, res):
        for ref, val in zip(acc, res):
            ref[rows, :] = val

    acc1 = (a1o, a1m, a1l)
    acc4 = (a4o, a4m, a4l)

    tasks = []
    for qa, qb, kbuf, vbuf, acc, per_class in ((q0, q1, ks, vs, acc1, nblk),
                                                (q40, q41, k4, v4, acc4, nblk // DIL_MID)):
        for b in range(nblk):
            rows = pl.ds(b * BAND, BAND)
            win = None if b % per_class == 0 else pl.ds((b - 1) * BAND, 2 * BAND)
            tasks.append((qa, qb, kbuf, vbuf, rows, win, acc, False))
    for cls in range(DIL_MID):
        for r in range(DIL_MID):
            rows = pl.ds(cls * (SEQ // DIL_MID) + r, BAND, stride=DIL_MID)
            tasks.append((q40, q41, k4, v4, rows, None, acc4, True))

    def scores(task):
        qa, qb, kbuf, _, rows, win, _, _ = task
        if win is None:
            return _band_scores(qa[rows, :], qb[rows, :], kbuf[rows, :], bias_self[...])
        return _band_scores(qa[rows, :], qb[rows, :], kbuf[win, :], bias_prev[...])

    def complete(task, s):
        _, _, _, vbuf, rows, win, acc, merge = task
        res = _band_output(s, vbuf[rows if win is None else win, :])
        if merge:
            res = _merge(tuple(ref[rows, :] for ref in acc), res)
        put(acc, rows, res)

    lookahead = 4
    pending = []
    for task in tasks:
        pending.append((task, scores(task)))
        if len(pending) > lookahead:
            complete(*pending.pop(0))
    for item in pending:
        complete(*item)

    for b in range(nblk):
        rows4 = pl.ds(b * BAND, BAND)
        rows1 = pl.ds(b // DIL_MID + (b % DIL_MID) * span, BAND, stride=DIL_MID)
        o, _, l = _merge(tuple(ref[rows1, :] for ref in acc1), tuple(ref[rows4, :] for ref in acc4))
        zbuf[rows1, :] = o * (1.0 / l) * zbuf[rows1, :]

    def finish(i, carry):
        rows = pl.ds(pl.multiple_of(i * BAND, BAND), BAND)
        out_ref[rows, :] = zbuf[rows, :].astype(BF16)
        return carry

    lax.fori_loop(0, nblk, finish, 0, unroll=4)


def _dilated_attention(qa, ka, va, za, rope_a, rope_b, rope_c, batch):
    slab = pl.BlockSpec((None, None, SEQ, LANES), lambda b, j: (b, j, 0, 0))
    table = pl.BlockSpec((SEQ, LANES), lambda b, j: (0, 0))
    return pl.pallas_call(
        _dilated_kernel,
        grid=(batch, ATTN_PAIRS),
        in_specs=[slab, slab, slab, slab, table, table,
                  pl.BlockSpec((LANES, LANES), lambda b, j: (0, 0))],
        out_specs=slab,
        out_shape=jax.ShapeDtypeStruct((batch, ATTN_PAIRS, SEQ, LANES), BF16),
        scratch_shapes=[pltpu.VMEM((SEQ, LANES), F32) for _ in range(15)] + [
            pltpu.VMEM((BAND, 2 * BAND), F32), pltpu.VMEM((BAND, BAND), F32)],
        compiler_params=pltpu.CompilerParams(
            dimension_semantics=("arbitrary", "arbitrary"), vmem_limit_bytes=VMEM_LIMIT_BYTES),
        name="dilated_attention",
    )(qa, ka, va, za, rope_a, rope_b, rope_c)


def _outproj_kernel(x_ref, ym_ref, ya_ref, px_ref, mem_ref, gmem_ref, wkv_ref, wout_ref,
                    gfin_ref, out_ref, kx_scr, vx_scr):
    @pl.when(pl.program_id(1) == 0)
    def _():
        mem = mem_ref[...]
        ms = jnp.mean(mem * mem, axis=-1, keepdims=True)
        mem_n = (mem * lax.rsqrt(ms + EPS) * gmem_ref[...]).astype(BF16)
        kv = _dot(mem_n, wkv_ref[...])
        kx_scr[...] = kv[:, :XATTN_WIDTH].astype(BF16)
        vx_scr[...] = kv[:, XATTN_WIDTH:].astype(BF16)

    scale = XATTN_HEAD_DIM ** -0.5 * LOG2E
    ones_b = jnp.ones((N_MEM, LANES), BF16)
    y_a = jnp.concatenate([ya_ref[j] for j in range(ATTN_PAIRS)], axis=1)
    y = _dot(y_a, wout_ref[MLSTM_WIDTH:MLSTM_WIDTH + ATTN_WIDTH, :])
    heads = []
    for h in range(XATTN_HEADS):
        sl = slice(h * XATTN_HEAD_DIM, (h + 1) * XATTN_HEAD_DIM)
        s = _dot_nt(px_ref[:, sl], kx_scr[:, sl]) * scale
        m = jnp.max(s, axis=1, keepdims=True)
        p = jnp.exp2(s - m).astype(BF16)
        o2 = _dot(p, jnp.concatenate([vx_scr[:, sl], ones_b], axis=1))
        heads.append(o2[:, :LANES] * (1.0 / o2[:, LANES:]))
        rows_m = slice(h * MLSTM_HEAD_DIM, (h + 1) * MLSTM_HEAD_DIM)
        y = y + _dot(ym_ref[:, rows_m], wout_ref[rows_m, :])
    o_x = jnp.concatenate(heads, axis=1)
    y_x = (o_x * _silu(px_ref[:, XATTN_WIDTH:].astype(F32))).astype(BF16)
    y = y + _dot(y_x, wout_ref[MLSTM_WIDTH + ATTN_WIDTH:, :])
    r = x_ref[...] + y
    ms = jnp.mean(r * r, axis=-1, keepdims=True)
    out_ref[...] = r * lax.rsqrt(ms + EPS) * gfin_ref[...]


def _output_projection(x2, y_m, y_a, px, mem, g_mem, wkv_b, wout_b, g_final, batch):
    tokens = x2.shape[0]
    tps = SEQ // ROW_TILE
    rows = lambda width: pl.BlockSpec((ROW_TILE, width), lambda b, i: (b * tps + i, 0))
    full = lambda shape: pl.BlockSpec(shape, lambda b, i: (0,) * len(shape))
    return pl.pallas_call(
        _outproj_kernel,
        grid=(batch, tps),
        in_specs=[
            rows(D_MODEL), rows(MLSTM_WIDTH),
            pl.BlockSpec((None, ATTN_PAIRS, ROW_TILE, LANES), lambda b, i: (b, 0, i, 0)),
            rows(2 * XATTN_WIDTH),
            pl.BlockSpec((None, N_MEM, D_MODEL), lambda b, i: (b, 0, 0)),
            full((1, D_MODEL)), full((D_MODEL, 2 * XATTN_WIDTH)), full((MIX_WIDTH, D_MODEL)),
            full((1, D_MODEL)),
        ],
        out_specs=rows(D_MODEL),
        out_shape=jax.ShapeDtypeStruct((tokens, D_MODEL), F32),
        scratch_shapes=[pltpu.VMEM((N_MEM, XATTN_WIDTH), BF16), pltpu.VMEM((N_MEM, XATTN_WIDTH), BF16)],
        compiler_params=pltpu.CompilerParams(
            dimension_semantics=("arbitrary", "arbitrary"), vmem_limit_bytes=VMEM_LIMIT_BYTES),
        name="out_projection",
    )(x2, y_m, y_a, px, mem, g_mem, wkv_b, wout_b, g_final)


def _block_diag_dense(w_blk):
    rows = w_blk.reshape(MLSTM_HEADS, MLSTM_HEAD_DIM, QKV_BLOCK)
    tiled = jnp.tile(rows, (1, 1, MLSTM_HEAD_DIM // QKV_BLOCK))
    idx = jnp.arange(MLSTM_HEAD_DIM) // QKV_BLOCK
    same_block = idx[:, None] == idx[None, :]
    return jnp.where(same_block[None], tiled, 0.0).astype(BF16)


def _rope_tables():
    half = ROPE_DIM // 2
    pos = jnp.arange(SEQ, dtype=F32)
    inv = ROPE_THETA ** (-jnp.arange(0, ROPE_DIM, 2, dtype=F32) / ROPE_DIM)
    ang = pos[:, None] * inv[None, :]
    cos, sin = jnp.cos(ang), jnp.sin(ang)
    ones = jnp.ones((SEQ, ATTN_HEAD_DIM - ROPE_DIM), F32)
    zeros = jnp.zeros((SEQ, ATTN_HEAD_DIM - ROPE_DIM), F32)
    zero_h = jnp.zeros((SEQ, half), F32)
    a = jnp.concatenate([cos, cos, ones], axis=1)
    d = jnp.concatenate([-sin, sin, zeros], axis=1)
    tile = lambda t: jnp.concatenate([t, t], axis=1)
    lane = jnp.arange(LANES)
    in_head = lane % ATTN_HEAD_DIM
    source = jnp.where(in_head < half, lane + half, lane - half)
    perm = (lane[:, None] == source[None, :]) & (in_head < ROPE_DIM)[None, :]
    return tile(a), tile(d), perm.astype(BF16)


def kernel(x, mem, g_norm, w_in, conv_w, conv_b, w_q_blk, w_k_blk, w_v_blk, w_gate, b_gate,
           g_head, skip, g_mem, w_mem_kv, w_out, g_final):
    batch = x.shape[0]
    assert x.shape[1:] == (SEQ, D_MODEL) and g_norm.shape[0] == 1
    layer = 0
    x2 = x.reshape(batch * SEQ, D_MODEL)
    wg_pad = jnp.pad(w_gate[layer], ((0, 0), (0, LANES - 2 * MLSTM_HEADS))).astype(BF16)
    bg_pad = jnp.pad(b_gate[layer], (0, LANES - 2 * MLSTM_HEADS))[None, :]
    qa, ka, va, za, px, y_m = _inproj_mlstm(
        x2, g_norm[layer][None, :], w_in[layer].astype(BF16), conv_w[layer], conv_b[layer][None, :],
        _block_diag_dense(w_q_blk[layer]), _block_diag_dense(w_k_blk[layer]),
        _block_diag_dense(w_v_blk[layer]), wg_pad, bg_pad,
        g_head[layer][None, :], skip[layer][None, :], batch)

    rope_a, rope_b, rope_c = _rope_tables()
    y_a = _dilated_attention(qa, ka, va, za, rope_a, rope_b, rope_c, batch)

    out = _output_projection(x2, y_m, y_a, px, mem,
                             g_mem[layer][None, :], w_mem_kv[layer].astype(BF16),
                             w_out[layer].astype(BF16), g_final[None, :], batch)
    return out.reshape(batch, SEQ, D_MODEL)
```

```python
import jax
import jax.numpy as jnp
from jax import lax
from jax.experimental import pallas as pl
from jax.experimental.pallas import tpu as pltpu

F32 = jnp.float32
BF16 = jnp.bfloat16

D_MODEL = 1024
SEQ = 2048
N_MEM = 256
EPS = 1e-6

MLSTM_HEADS = 4
MLSTM_HEAD_DIM = 256
MLSTM_WIDTH = 1024
QKV_BLOCK = 4
CONV_WIDTH = 4

ATTN_HEADS = 8
ATTN_HEAD_DIM = 64
ATTN_WIDTH = 512
ATTN_PAIRS = ATTN_HEADS // 2
ROPE_DIM = 16
ROPE_THETA = 500000.0
BAND = 128
DIL_MID = 4
DIL_MAX = 16

XATTN_HEADS = 4
XATTN_HEAD_DIM = 128
XATTN_WIDTH = 512

MIX_WIDTH = MLSTM_WIDTH + ATTN_WIDTH + XATTN_WIDTH
IN_WIDTH = 3 * MLSTM_WIDTH + 4 * ATTN_WIDTH + 2 * XATTN_WIDTH

LANES = 128
VMEM_LIMIT_BYTES = 56 * 1024 * 1024
NEG = -1e30
LOG2E = 1.4426950408889634

ROW_TILE = 1024
PROJ_COLS = 512
MLSTM_CHUNK = 256
MEMKV_SEQS = 4


def _sigmoid(x):
    return 1.0 / (1.0 + jnp.exp2(x * (-LOG2E)))


def _silu(x):
    return x * _sigmoid(x)


def _log_sigmoid(x):
    return jnp.minimum(x, 0.0) - jnp.log(1.0 + jnp.exp(-jnp.abs(x)))


def _dot(a, b):
    return jnp.dot(a, b, preferred_element_type=F32)


def _dot_nt(a, b):
    return lax.dot_general(a, b, (((1,), (1,)), ((), ())), preferred_element_type=F32)


def _dot_tn(a, b):
    return lax.dot_general(a, b, (((0,), (0,)), ((), ())), preferred_element_type=F32)


def _mlstm_front(xm_ref, convw_ref, convb_ref, wq_ref, wk_ref, wv_ref, wg_ref, bg_ref, xbuf, stage):
    L = MLSTM_CHUNK
    dh = MLSTM_HEAD_DIM
    xc_ref, q_ref, k_ref, v_ref, gcol_ref, grow_ref = stage

    xbuf[8:8 + L, :] = xm_ref[...].astype(F32)
    conv = convb_ref[...] + convw_ref[3:4, :] * xbuf[8:8 + L, :]
    conv = conv + convw_ref[2:3, :] * xbuf[7:7 + L, :]
    conv = conv + convw_ref[1:2, :] * xbuf[6:6 + L, :]
    conv = conv + convw_ref[0:1, :] * xbuf[5:5 + L, :]
    xbuf[0:8, :] = xbuf[L:L + 8, :]
    xc = _silu(conv)
    xc_ref[...] = xc
    xc_b = xc.astype(BF16)
    yield

    k_scale = dh ** -0.5
    q_b, k_b, v_b = [], [], []
    for h in range(MLSTM_HEADS):
        sl = slice(h * dh, (h + 1) * dh)
        q_b.append(_dot(xc_b[:, sl], wq_ref[h]).astype(BF16))
        k = _dot(xc_b[:, sl], wk_ref[h])
        k_ref[:, sl] = k * k_scale
        k_b.append(k.astype(BF16))
        v_b.append(_dot(xm_ref[:, sl], wv_ref[h]).astype(BF16))
        q_ref[:, sl] = q_b[h]
        v_ref[:, sl] = v_b[h]
        yield

    qkv_b = jnp.concatenate(q_b + k_b + v_b, axis=1)
    gates = _dot(qkv_b, wg_ref[...]) + bg_ref[...]
    causal = lax.broadcasted_iota(jnp.int32, (L, L), 0) >= lax.broadcasted_iota(jnp.int32, (L, L), 1)
    b2 = jnp.dot(causal.astype(F32), _log_sigmoid(gates) * LOG2E, precision=lax.Precision.HIGHEST,
                 preferred_element_type=F32)
    w2 = gates * LOG2E - pltpu.roll(b2, LANES - MLSTM_HEADS, 1)
    lane = lax.broadcasted_iota(jnp.int32, (L, LANES), 1)
    gcol = jnp.where(lane < MLSTM_HEADS, w2, b2)
    gcol_ref[...] = gcol
    grow_ref[...] = gcol.T


def _mlstm_back(stage, zm_ref, om_ref, ghead_ref, skip_ref, y_ref, cn_scr, m_scr):
    L = MLSTM_CHUNK
    dh = MLSTM_HEAD_DIM
    xc_ref, q_ref, k_ref, v_ref, gcol_ref, grow_ref = stage
    causal = lax.broadcasted_iota(jnp.int32, (L, L), 0) >= lax.broadcasted_iota(jnp.int32, (L, L), 1)
    ones_b = jnp.ones((L, LANES), BF16)

    for h in range(MLSTM_HEADS):
        sl = slice(h * dh, (h + 1) * dh)
        w_col = gcol_ref[:, h:h + 1]
        b_col = gcol_ref[:, MLSTM_HEADS + h:MLSTM_HEADS + h + 1]
        w_row = grow_ref[h:h + 1, :]
        m_prev = m_scr[h, 0:1, 0:1]
        cn_old = cn_scr[h]
        q_b = q_ref[:, sl]
        k_f = k_ref[:, sl]
        v_aug = jnp.concatenate([v_ref[:, sl], ones_b], axis=1)

        mw = jnp.where(causal, w_row, NEG)
        mm = jnp.maximum(m_prev, jnp.max(mw, axis=1, keepdims=True))
        wts = (jnp.exp2(mw - mm) * _dot_nt(q_b, k_f.astype(BF16))).astype(BF16)
        inter = jnp.broadcast_to(jnp.exp2(m_prev - mm), (L, LANES))
        inter_c = _dot(q_b, cn_old.astype(BF16))
        intra = _dot(wts, v_aug)
        den = inter * inter_c[:, dh:] + intra[:, dh:]
        floor = jnp.broadcast_to(jnp.exp2(-(b_col + mm)), (L, LANES))
        r = 1.0 / jnp.maximum(jnp.abs(den), floor)
        h_t = (jnp.concatenate([inter, inter], axis=1) * inter_c[:, :dh] + intra[:, :dh]) * \
            jnp.concatenate([r, r], axis=1)

        mm_last = mm[L - 1:L, :]
        gk = (jnp.exp2(w_col - mm_last) * k_f).astype(BF16)
        decay = jnp.exp2(m_prev - mm_last)
        cn_scr[h] = decay * cn_old + _dot_tn(gk, v_aug)
        m_scr[h] = jnp.broadcast_to(b_col[L - 1:L, :] + mm_last, (8, LANES))
        yield

        hg = _sigmoid(om_ref[:, sl].astype(F32)) * h_t
        mu = jnp.mean(hg, axis=1, keepdims=True)
        cen = hg - mu
        var = jnp.mean(cen * cen, axis=1, keepdims=True)
        ln = cen * lax.rsqrt(var + EPS) * ghead_ref[:, sl]
        y = (ln + skip_ref[:, sl] * xc_ref[:, sl]) * _silu(zm_ref[:, sl].astype(F32))
        y_ref[:, sl] = y.astype(BF16)
        yield


def _round_robin(*gens):
    done = object()
    live = list(gens)
    while live:
        live = [g for g in live if next(g, done) is not done]


def _inproj_pieces(h_scr, w_ref, sinks):
    for col, store in sinks:
        store(_dot(h_scr[...], w_ref[:, col:col + PROJ_COLS]).astype(BF16))
        yield


def _inproj_mlstm_kernel(x_ref, g_ref, w_ref, convw_ref, convb_ref, wq_ref, wk_ref, wv_ref,
                         wg_ref, bg_ref, ghead_ref, skip_ref,
                         qa_ref, ka_ref, va_ref, za_ref, px_ref, ym_ref,
                         h_scr, pm_cur, xm_prev, zo_prev, xbuf, cn_scr, m_scr, *stages):
    L = MLSTM_CHUNK
    steps_per_seq = SEQ // (2 * L)
    j = pl.program_id(0)
    stage0, stage1 = stages[:len(stages) // 2], stages[len(stages) // 2:]
    front_args = (convw_ref, convb_ref, wq_ref, wk_ref, wv_ref, wg_ref, bg_ref, xbuf)
    back_args = (ghead_ref, skip_ref)

    @pl.when(j == 0)
    def _():
        xbuf[0:8, :] = jnp.zeros((8, MLSTM_WIDTH), F32)
        xm_prev[...] = jnp.zeros_like(xm_prev)
        zo_prev[...] = jnp.zeros_like(zo_prev)
        for ref in stage0:
            ref[...] = jnp.zeros_like(ref)

    @pl.when((j == 0) | (j % steps_per_seq == 1))
    def _():
        cn_scr[...] = jnp.zeros_like(cn_scr)
        m_scr[...] = jnp.zeros_like(m_scr)

    x = x_ref[...]
    ms = jnp.mean(x * x, axis=-1, keepdims=True)
    h_scr[...] = (x * lax.rsqrt(ms + EPS) * g_ref[...]).astype(BF16)

    def pm_sink(col):
        def store(r):
            pm_cur[:, col:col + PROJ_COLS] = r
        return store

    def pair_sink(ref, col):
        def store(r):
            for p in range(PROJ_COLS // LANES):
                ref[col // LANES + p] = r[:, p * LANES:(p + 1) * LANES]
        return store

    def px_sink(col):
        def store(r):
            px_ref[:, col:col + PROJ_COLS] = r
        return store

    base_a = 3 * MLSTM_WIDTH
    base_x = base_a + 4 * ATTN_WIDTH
    sinks_a = [(n * PROJ_COLS, pm_sink(n * PROJ_COLS)) for n in range(3 * MLSTM_WIDTH // PROJ_COLS)]
    sinks_b = [(base_a + k * ATTN_WIDTH + col, pair_sink(ref, col))
               for k, ref in enumerate((qa_ref, ka_ref, va_ref, za_ref))
               for col in range(0, ATTN_WIDTH, PROJ_COLS)]
    sinks_b += [(base_x + n * PROJ_COLS, px_sink(n * PROJ_COLS))
                for n in range(2 * XATTN_WIDTH // PROJ_COLS)]

    first, second = pl.ds(0, L), pl.ds(L, L)
    zm_cols, om_cols = pl.ds(0, MLSTM_WIDTH), pl.ds(MLSTM_WIDTH, MLSTM_WIDTH)
    _round_robin(
        _inproj_pieces(h_scr, w_ref, sinks_a),
        _mlstm_front(xm_prev, *front_args, stage1),
        _mlstm_back(stage0, zo_prev.at[first, zm_cols], zo_prev.at[first, om_cols], *back_args,
                    ym_ref.at[first], cn_scr, m_scr))

    @pl.when(j % steps_per_seq == 0)
    def _():
        xbuf[0:8, :] = jnp.zeros((8, MLSTM_WIDTH), F32)

    _round_robin(
        _inproj_pieces(h_scr, w_ref, sinks_b),
        _mlstm_front(pm_cur.at[first, pl.ds(0, MLSTM_WIDTH)], *front_args, stage0),
        _mlstm_back(stage1, zo_prev.at[second, zm_cols], zo_prev.at[second, om_cols], *back_args,
                    ym_ref.at[second], cn_scr, m_scr))

    xm_prev[...] = pm_cur[L:2 * L, 0:MLSTM_WIDTH]
    zo_prev[...] = pm_cur[:, MLSTM_WIDTH:3 * MLSTM_WIDTH]


def _inproj_mlstm(x2, g_norm, w_in_b, conv_w, conv_b, wq_d, wk_d, wv_d, wg_pad, bg_pad, g_head, skip,
                  batch):
    L = MLSTM_CHUNK
    tile = 2 * L
    tokens = x2.shape[0]
    tiles = tokens // tile
    tiles_per_seq = SEQ // tile
    assert SEQ % tile == 0
    dh = MLSTM_HEAD_DIM
    cur = lambda j: jnp.minimum(j, tiles - 1)
    full = lambda shape: pl.BlockSpec(shape, lambda j: (0,) * len(shape))
    pair_spec = pl.BlockSpec((None, ATTN_PAIRS, tile, LANES),
                             lambda j: (cur(j) // tiles_per_seq, 0, cur(j) % tiles_per_seq, 0))
    pair_shape = jax.ShapeDtypeStruct((batch, ATTN_PAIRS, SEQ, LANES), BF16)
    stage = [pltpu.VMEM((L, MLSTM_WIDTH), F32), pltpu.VMEM((L, MLSTM_WIDTH), BF16),
             pltpu.VMEM((L, MLSTM_WIDTH), F32), pltpu.VMEM((L, MLSTM_WIDTH), BF16),
             pltpu.VMEM((L, LANES), F32), pltpu.VMEM((LANES, L), F32)]
    return pl.pallas_call(
        _inproj_mlstm_kernel,
        grid=(tiles + 1,),
        in_specs=[
            pl.BlockSpec((tile, D_MODEL), lambda j: (cur(j), 0)),
            full((1, D_MODEL)),
            pl.BlockSpec((D_MODEL, IN_WIDTH), lambda j: (0, 0), pipeline_mode=pl.Buffered(1)),
            full((CONV_WIDTH, MLSTM_WIDTH)), full((1, MLSTM_WIDTH)),
            full((MLSTM_HEADS, dh, dh)), full((MLSTM_HEADS, dh, dh)), full((MLSTM_HEADS, dh, dh)),
            full((3 * MLSTM_WIDTH, LANES)), full((1, LANES)),
            full((1, MLSTM_WIDTH)), full((1, MLSTM_WIDTH)),
        ],
        out_specs=[
            pair_spec, pair_spec, pair_spec, pair_spec,
            pl.BlockSpec((tile, 2 * XATTN_WIDTH), lambda j: (cur(j), 0)),
            pl.BlockSpec((tile, MLSTM_WIDTH), lambda j: (jnp.maximum(j - 1, 0), 0)),
        ],
        out_shape=[
            pair_shape, pair_shape, pair_shape, pair_shape,
            jax.ShapeDtypeStruct((tokens, 2 * XATTN_WIDTH), BF16),
            jax.ShapeDtypeStruct((tokens, MLSTM_WIDTH), BF16),
        ],
        scratch_shapes=[
            pltpu.VMEM((tile, D_MODEL), BF16),
            pltpu.VMEM((tile, 3 * MLSTM_WIDTH), BF16),
            pltpu.VMEM((L, MLSTM_WIDTH), BF16),
            pltpu.VMEM((tile, 2 * MLSTM_WIDTH), BF16),
            pltpu.VMEM((L + 8, MLSTM_WIDTH), F32),
            pltpu.VMEM((MLSTM_HEADS, dh, dh + LANES), F32),
            pltpu.VMEM((MLSTM_HEADS, 8, LANES), F32),
        ] + stage + stage,
        compiler_params=pltpu.CompilerParams(
            dimension_semantics=("arbitrary",), vmem_limit_bytes=VMEM_LIMIT_BYTES),
        name="in_projection_mlstm",
    )(x2, g_norm, w_in_b, conv_w, conv_b, wq_d, wk_d, wv_d, wg_pad, bg_pad, g_head, skip)


def _band_scores(q_top, q_bot, kk, bias):
    q2 = jnp.concatenate([q_top, q_bot], axis=0).astype(BF16)
    return _dot_nt(q2, kk.astype(BF16)) + jnp.concatenate([bias, bias], axis=0)


def _band_output(s, vv):
    nk = vv.shape[0]
    m = jnp.max(s, axis=1, keepdims=True)
    p = jnp.exp2(s - m).astype(BF16)
    v_aug = jnp.concatenate([vv.astype(BF16), jnp.ones((nk, LANES), BF16)], axis=1)
    o2 = _dot(p, v_aug)
    first = lax.broadcasted_iota(jnp.int32, (BAND, LANES), 1) < ATTN_HEAD_DIM
    o = jnp.where(first, o2[:BAND, :LANES], o2[BAND:, :LANES])
    l_p = jnp.where(first, o2[:BAND, LANES:], o2[BAND:, LANES:])
    m_p = jnp.where(first, m[:BAND], m[BAND:])
    return o, m_p, l_p


def _merge(a, b):
    (o_a, m_a, l_a), (o_b, m_b, l_b) = a, b
    m_new = jnp.maximum(m_a, m_b)
    w_a = jnp.exp2(m_a - m_new)
    w_b = jnp.exp2(m_b - m_new)
    return o_a * w_a + o_b * w_b, m_new, l_a * w_a + l_b * w_b


def _dilated_kernel(q_ref, k_ref, v_ref, z_ref, ra_ref, rb_ref, rc_ref, out_ref,
                    q0, q1, ks, vs, q40, q41, k4, v4, a1o, a1m, a1l, a4o, a4m, a4l, zbuf,
                    bias_prev, bias_self):
    nblk = SEQ // BAND
    span = DIL_MID * BAND
    q_scale = ATTN_HEAD_DIM ** -0.5 * LOG2E

    t = lax.broadcasted_iota(jnp.int32, (BAND, 2 * BAND), 0)
    c = lax.broadcasted_iota(jnp.int32, (BAND, 2 * BAND), 1)
    bias_prev[...] = jnp.where((c >= t) & (c <= t + BAND), 0.0, NEG)
    t = lax.broadcasted_iota(jnp.int32, (BAND, BAND), 0)
    c = lax.broadcasted_iota(jnp.int32, (BAND, BAND), 1)
    bias_self[...] = jnp.where(c <= t, 0.0, NEG)

    def rope(x_b, rows):
        partner = _dot(x_b, rc_ref[...])
        return x_b.astype(F32) * ra_ref[rows, :] + partner * rb_ref[rows, :]

    def prep(i, carry):
        rows = pl.ds(pl.multiple_of(i * BAND, BAND), BAND)
        q = rope(q_ref[rows, :], rows) * q_scale
        first = lax.broadcasted_iota(jnp.int32, (BAND, LANES), 1) < ATTN_HEAD_DIM
        q0[rows, :] = jnp.where(first, q, 0.0)
        q1[rows, :] = jnp.where(first, 0.0, q)
        ks[rows, :] = rope(k_ref[rows, :], rows)
        vs[rows, :] = v_ref[rows, :].astype(F32)
        zbuf[rows, :] = _silu(z_ref[rows, :].astype(F32))
        return carry

    lax.fori_loop(0, nblk, prep, 0, unroll=8)

    def deinterleave(idx, carry):
        src = pl.ds(idx // DIL_MID + (idx % DIL_MID) * span, BAND, stride=DIL_MID)
        dst = pl.ds(pl.multiple_of(idx * BAND, BAND), BAND)
        q40[dst, :] = q0[src, :]
        q41[dst, :] = q1[src, :]
        k4[dst, :] = ks[src, :]
        v4[dst, :] = vs[src, :]
        return carry

    lax.fori_loop(0, nblk, deinterleave, 0, unroll=True)

    def put(acc, rows, res):
        for ref, val in zip(acc, res):
            ref[rows, :] = val

    acc1 = (a1o, a1m, a1l)
    acc4 = (a4o, a4m, a4l)

    tasks = []
    for qa, qb, kbuf, vbuf, acc, per_class in ((q0, q1, ks, vs, acc1, nblk),
                                                (q40, q41, k4, v4, acc4, nblk // DIL_MID)):
        for b in range(nblk):
            rows = pl.ds(b * BAND, BAND)
            win = None if b % per_class == 0 else pl.ds((b - 1) * BAND, 2 * BAND)
            tasks.append((qa, qb, kbuf, vbuf, rows, win, acc, False))
    for cls in range(DIL_MID):
        for r in range(DIL_MID):
            rows = pl.ds(cls * (SEQ // DIL_MID) + r, BAND, stride=DIL_MID)
            tasks.append((q40, q41, k4, v4, rows, None, acc4, True))

    def scores(task):
        qa, qb, kbuf, _, rows, win, _, _ = task
        if win is None:
            return _band_scores(qa[rows, :], qb[rows, :], kbuf[rows, :], bias_self[...])
        return _band_scores(qa[rows, :], qb[rows, :], kbuf[win, :], bias_prev[...])

    def complete(task, s):
        _, _, _, vbuf, rows, win, acc, merge = task
        res = _band_output(s, vbuf[rows if win is None else win, :])
        if merge:
            res = _merge(tuple(ref[rows, :] for ref in acc), res)
        put(acc, rows, res)

    lookahead = 4
    pending = []
    for task in tasks:
        pending.append((task, scores(task)))
        if len(pending) > lookahead:
            complete(*pending.pop(0))
    for item in pending:
        complete(*item)

    for b in range(nblk):
        rows4 = pl.ds(b * BAND, BAND)
        rows1 = pl.ds(b // DIL_MID + (b % DIL_MID) * span, BAND, stride=DIL_MID)
        o, _, l = _merge(tuple(ref[rows1, :] for ref in acc1), tuple(ref[rows4, :] for ref in acc4))
        zbuf[rows1, :] = o * (1.0 / l) * zbuf[rows1, :]

    def finish(i, carry):
        rows = pl.ds(pl.multiple_of(i * BAND, BAND), BAND)
        out_ref[rows, :] = zbuf[rows, :].astype(BF16)
        return carry

    lax.fori_loop(0, nblk, finish, 0, unroll=4)


def _dilated_attention(qa, ka, va, za, rope_a, rope_b, rope_c, batch):
    slab = pl.BlockSpec((None, None, SEQ, LANES), lambda b, j: (b, j, 0, 0))
    table = pl.BlockSpec((SEQ, LANES), lambda b, j: (0, 0))
    return pl.pallas_call(
        _dilated_kernel,
        grid=(batch, ATTN_PAIRS),
        in_specs=[slab, slab, slab, slab, table, table,
                  pl.BlockSpec((LANES, LANES), lambda b, j: (0, 0))],
        out_specs=slab,
        out_shape=jax.ShapeDtypeStruct((batch, ATTN_PAIRS, SEQ, LANES), BF16),
        scratch_shapes=[pltpu.VMEM((SEQ, LANES), F32) for _ in range(15)] + [
            pltpu.VMEM((BAND, 2 * BAND), F32), pltpu.VMEM((BAND, BAND), F32)],
        compiler_params=pltpu.CompilerParams(
            dimension_semantics=("arbitrary", "arbitrary"), vmem_limit_bytes=VMEM_LIMIT_BYTES),
        name="dilated_attention",
    )(qa, ka, va, za, rope_a, rope_b, rope_c)


def _memkv_kernel(mem_ref, gmem_ref, wkv_ref, kv_ref):
    mem = mem_ref[...]
    ms = jnp.mean(mem * mem, axis=-1, keepdims=True)
    mem_n = (mem * lax.rsqrt(ms + EPS) * gmem_ref[...]).astype(BF16)
    kv_ref[...] = _dot(mem_n, wkv_ref[...]).astype(BF16)


def _memory_kv(mem2, g_mem, wkv_b):
    rows = MEMKV_SEQS * N_MEM
    return pl.pallas_call(
        _memkv_kernel,
        grid=(mem2.shape[0] // rows,),
        in_specs=[pl.BlockSpec((rows, D_MODEL), lambda i: (i, 0)),
                  pl.BlockSpec((1, D_MODEL), lambda i: (0, 0)),
                  pl.BlockSpec((D_MODEL, 2 * XATTN_WIDTH), lambda i: (0, 0))],
        out_specs=pl.BlockSpec((rows, 2 * XATTN_WIDTH), lambda i: (i, 0)),
        out_shape=jax.ShapeDtypeStruct((mem2.shape[0], 2 * XATTN_WIDTH), BF16),
        compiler_params=pltpu.CompilerParams(
            dimension_semantics=("arbitrary",), vmem_limit_bytes=VMEM_LIMIT_BYTES),
        name="memory_kv",
    )(mem2, g_mem, wkv_b)


def _outproj_kernel(x_ref, ym_ref, ya_ref, px_ref, kv_ref, wout_ref, gfin_ref, out_ref):
    kx_scr = kv_ref.at[:, pl.ds(0, XATTN_WIDTH)]
    vx_scr = kv_ref.at[:, pl.ds(XATTN_WIDTH, XATTN_WIDTH)]
    scale = XATTN_HEAD_DIM ** -0.5 * LOG2E
    ones_b = jnp.ones((N_MEM, LANES), BF16)
    y_a = jnp.concatenate([ya_ref[j] for j in range(ATTN_PAIRS)], axis=1)
    y = _dot(y_a, wout_ref[MLSTM_WIDTH:MLSTM_WIDTH + ATTN_WIDTH, :])
    heads = []
    for h in range(XATTN_HEADS):
        sl = slice(h * XATTN_HEAD_DIM, (h + 1) * XATTN_HEAD_DIM)
        s = _dot_nt(px_ref[:, sl], kx_scr[:, sl]) * scale
        m = jnp.max(s, axis=1, keepdims=True)
        p = jnp.exp2(s - m).astype(BF16)
        o2 = _dot(p, jnp.concatenate([vx_scr[:, sl], ones_b], axis=1))
        heads.append(o2[:, :LANES] * (1.0 / o2[:, LANES:]))
        rows_m = slice(h * MLSTM_HEAD_DIM, (h + 1) * MLSTM_HEAD_DIM)
        y = y + _dot(ym_ref[:, rows_m], wout_ref[rows_m, :])
    o_x = jnp.concatenate(heads, axis=1)
    y_x = (o_x * _silu(px_ref[:, XATTN_WIDTH:].astype(F32))).astype(BF16)
    y = y + _dot(y_x, wout_ref[MLSTM_WIDTH + ATTN_WIDTH:, :])
    r = x_ref[...] + y
    ms = jnp.mean(r * r, axis=-1, keepdims=True)
    out_ref[...] = r * lax.rsqrt(ms + EPS) * gfin_ref[...]


def _output_projection(x2, y_m, y_a, px, mem, g_mem, wkv_b, wout_b, g_final, batch):
    tokens = x2.shape[0]
    tps = SEQ // ROW_TILE
    rows = lambda width: pl.BlockSpec((ROW_TILE, width), lambda b, i: (b * tps + i, 0))
    full = lambda shape: pl.BlockSpec(shape, lambda b, i: (0,) * len(shape))
    return pl.pallas_call(
        _outproj_kernel,
        grid=(batch, tps),
        in_specs=[
            rows(D_MODEL), rows(MLSTM_WIDTH),
            pl.BlockSpec((None, ATTN_PAIRS, ROW_TILE, LANES), lambda b, i: (b, 0, i, 0)),
            rows(2 * XATTN_WIDTH),
            pl.BlockSpec((N_MEM, 2 * XATTN_WIDTH), lambda b, i: (b, 0)),
            full((MIX_WIDTH, D_MODEL)), full((1, D_MODEL)),
        ],
        out_specs=rows(D_MODEL),
        out_shape=jax.ShapeDtypeStruct((tokens, D_MODEL), F32),
        compiler_params=pltpu.CompilerParams(
            dimension_semantics=("arbitrary", "arbitrary"), vmem_limit_bytes=VMEM_LIMIT_BYTES),
        name="out_projection",
    )(x2, y_m, y_a, px, _memory_kv(mem.reshape(-1, D_MODEL), g_mem, wkv_b), wout_b, g_final)


def _block_diag_dense(w_blk):
    rows = w_blk.reshape(MLSTM_HEADS, MLSTM_HEAD_DIM, QKV_BLOCK)
    tiled = jnp.tile(rows, (1, 1, MLSTM_HEAD_DIM // QKV_BLOCK))
    idx = jnp.arange(MLSTM_HEAD_DIM) // QKV_BLOCK
    same_block = idx[:, None] == idx[None, :]
    return jnp.where(same_block[None], tiled, 0.0).astype(BF16)


def _rope_tables():
    half = ROPE_DIM // 2
    pos = jnp.arange(SEQ, dtype=F32)
    inv = ROPE_THETA ** (-jnp.arange(0, ROPE_DIM, 2, dtype=F32) / ROPE_DIM)
    ang = pos[:, None] * inv[None, :]
    cos, sin = jnp.cos(ang), jnp.sin(ang)
    ones = jnp.ones((SEQ, ATTN_HEAD_DIM - ROPE_DIM), F32)
    zeros = jnp.zeros((SEQ, ATTN_HEAD_DIM - ROPE_DIM), F32)
    zero_h = jnp.zeros((SEQ, half), F32)
    a = jnp.concatenate([cos, cos, ones], axis=1)
    d = jnp.concatenate([-sin, sin, zeros], axis=1)
    tile = lambda t: jnp.concatenate([t, t], axis=1)
    lane = jnp.arange(LANES)
    in_head = lane % ATTN_HEAD_DIM
    source = jnp.where(in_head < half, lane + half, lane - half)
    perm = (lane[:, None] == source[None, :]) & (in_head < ROPE_DIM)[None, :]
    return tile(a), tile(d), perm.astype(BF16)


def kernel(x, mem, g_norm, w_in, conv_w, conv_b, w_q_blk, w_k_blk, w_v_blk, w_gate, b_gate,
           g_head, skip, g_mem, w_mem_kv, w_out, g_final):
    batch = x.shape[0]
    assert x.shape[1:] == (SEQ, D_MODEL) and g_norm.shape[0] == 1
    layer = 0
    x2 = x.reshape(batch * SEQ, D_MODEL)
    wg_pad = jnp.pad(w_gate[layer], ((0, 0), (0, LANES - 2 * MLSTM_HEADS))).astype(BF16)
    bg_pad = jnp.pad(b_gate[layer], (0, LANES - 2 * MLSTM_HEADS))[None, :]
    qa, ka, va, za, px, y_m = _inproj_mlstm(
        x2, g_norm[layer][None, :], w_in[layer].astype(BF16), conv_w[layer], conv_b[layer][None, :],
        _block_diag_dense(w_q_blk[layer]), _block_diag_dense(w_k_blk[layer]),
        _block_diag_dense(w_v_blk[layer]), wg_pad, bg_pad,
        g_head[layer][None, :], skip[layer][None, :], batch)

    rope_a, rope_b, rope_c = _rope_tables()
    y_a = _dilated_attention(qa, ka, va, za, rope_a, rope_b, rope_c, batch)

    out = _output_projection(x2, y_m, y_a, px, mem,
                             g_mem[layer][None, :], w_mem_kv[layer].astype(BF16),
                             w_out[layer].astype(BF16), g_final[None, :], batch)
    return out.reshape(batch, SEQ, D_MODEL)
```
